```python
import math
import jax, jax.numpy as jnp
from jax import lax
import numpy as np

D_MODEL = 1024
BATCH = 8
SEQ = 2048
DEPTH = 4
DEC_BATCH = 16
DEC_SEQ = 4096
PAST_LEN = 128

GRID_W = 64
Q_BLOCK = 128
RMS_EPS = 1e-6
ROPE_THETA = 500000.0
AXIAL_THETA = 10000.0

A_HEADS = 8
A_KV_HEADS = 2
A_HEAD_DIM = 64
A_AXIS_DIM = A_HEAD_DIM // 2
A_WIDTH = A_HEADS * A_HEAD_DIM
B_HEADS = 4
B_HEAD_DIM = 64
B_V_DIM = 2 * B_HEAD_DIM
B_ROPE_DIM = B_HEAD_DIM // 4
B_WIDTH = B_HEADS * B_V_DIM
C_HEADS = 8
C_Q_RANK = 384
C_KV_RANK = 256
C_NOPE_DIM = 64
C_ROPE_DIM = 32
C_V_DIM = 64
C_WIDTH = C_HEADS * C_V_DIM
N_BRANCH = 3
A_Q_COLS = A_HEADS * A_HEAD_DIM
A_KV_COLS = A_KV_HEADS * A_HEAD_DIM
B_QK_COLS = B_HEADS * 2 * B_HEAD_DIM
B_V_COLS = B_HEADS * B_V_DIM
GATE_COLS = N_BRANCH * D_MODEL
SPLITS = (A_Q_COLS, A_KV_COLS, A_KV_COLS, B_QK_COLS, B_QK_COLS, B_V_COLS, C_Q_RANK, C_KV_RANK, C_ROPE_DIM, GATE_COLS)
IN_WIDTH = sum(SPLITS)
PEER_HEADS = 8
PEER_N_KEYS = 128
PEER_N_EXPERTS = PEER_N_KEYS * PEER_N_KEYS
PEER_TOPK = 16
PEER_KEY_DIM = 256
PEER_HALF = PEER_KEY_DIM // 2
PEER_CHUNK = 128

kernel_name = "hybrid_gated_gqa_diff_mla_peer_encoder"


def rms_norm(x, g):
    xf = x.astype(jnp.float32)
    y = xf * lax.rsqrt(jnp.mean(xf * xf, axis=-1, keepdims=True) + RMS_EPS)
    return (y * g.astype(jnp.float32)).astype(x.dtype)


def rope_angles(pos, dim, theta):
    inv = 1.0 / (theta ** (jnp.arange(0, dim, 2, dtype=jnp.float32) / dim))
    ang = pos[:, None] * inv[None, :]
    return jnp.cos(ang), jnp.sin(ang)


def apply_rope(x, cs):
    cos, sin = cs
    extra = x.ndim - 3
    shp = (cos.shape[0],) + (1,) * extra + (cos.shape[1],)
    cos = cos.reshape(shp).astype(x.dtype)
    sin = sin.reshape(shp).astype(x.dtype)
    half = x.shape[-1] // 2
    x1, x2 = x[..., :half], x[..., half:]
    return jnp.concatenate([x1 * cos - x2 * sin, x2 * cos + x1 * sin], axis=-1)


def sweep_query_blocks(q, block_fn):
    b, s = q.shape[:2]
    nb = s // Q_BLOCK
    qb = jnp.swapaxes(q.reshape((b, nb, Q_BLOCK) + q.shape[2:]), 0, 1)
    out = lax.map(block_fn, qb)
    return jnp.swapaxes(out, 0, 1).reshape((b, s) + out.shape[3:])


def gqa_attention(q, k, v, scale):
    def block(qb):
        sc = jnp.einsum('bqhgd,bshd->bhgqs', qb, k).astype(jnp.float32) * scale
        p = jax.nn.softmax(sc, axis=-1).astype(v.dtype)
        return jnp.einsum('bhgqs,bshe->bqhge', p, v)
    return sweep_query_blocks(q, block)


def diff_attention(q, k, v, lam, scale):
    def block(qb):
        sc = jnp.einsum('bqhcd,bshcd->bhcqs', qb, k).astype(jnp.float32) * scale
        p = jax.nn.softmax(sc, axis=-1)
        a = (p[:, :, 0] - lam * p[:, :, 1]).astype(v.dtype)
        return jnp.einsum('bhqs,bshe->bqhe', a, v)
    return sweep_query_blocks(q, block)


def token_mixing(h, p, i, axial_row, axial_col, rope_b, rope_c):
    b, s, _ = h.shape
    proj = h @ p['w_in'][i]
    parts = []
    off = 0
    for w in SPLITS:
        parts.append(proj[..., off:off + w])
        off += w
    qa = parts[0].reshape(b, s, A_KV_HEADS, A_HEADS // A_KV_HEADS, A_HEAD_DIM)
    ka = parts[1].reshape(b, s, A_KV_HEADS, A_HEAD_DIM)
    va = parts[2].reshape(b, s, A_KV_HEADS, A_HEAD_DIM)
    qa = rms_norm(qa, p['a_q_norm_g'][i])
    ka = rms_norm(ka, p['a_k_norm_g'][i])
    qa = jnp.concatenate([apply_rope(qa[..., :A_AXIS_DIM], axial_row), apply_rope(qa[..., A_AXIS_DIM:], axial_col)], axis=-1)
    ka = jnp.concatenate([apply_rope(ka[..., :A_AXIS_DIM], axial_row), apply_rope(ka[..., A_AXIS_DIM:], axial_col)], axis=-1)
    ya = gqa_attention(qa, ka, va, A_HEAD_DIM ** -0.5).reshape(b, s, A_WIDTH)
    qb = parts[3].reshape(b, s, B_HEADS, 2, B_HEAD_DIM)
    kb = parts[4].reshape(b, s, B_HEADS, 2, B_HEAD_DIM)
    vb = parts[5].reshape(b, s, B_HEADS, B_V_DIM)
    qb = jnp.concatenate([apply_rope(qb[..., :B_ROPE_DIM], rope_b), qb[..., B_ROPE_DIM:]], axis=-1)
    kb = jnp.concatenate([apply_rope(kb[..., :B_ROPE_DIM], rope_b), kb[..., B_ROPE_DIM:]], axis=-1)
    lam_init = 0.8 - 0.6 * math.exp(-0.3 * i)
    f32 = jnp.float32
    lam = (jnp.exp(jnp.sum(p['b_lambda_q1'][i].astype(f32) * p['b_lambda_k1'][i].astype(f32)))
           - jnp.exp(jnp.sum(p['b_lambda_q2'][i].astype(f32) * p['b_lambda_k2'][i].astype(f32))) + lam_init)
    ob = diff_attention(qb, kb, vb, lam, B_HEAD_DIM ** -0.5)
    ob = rms_norm(ob, p['b_subln_g'][i]) * (1.0 - lam_init)
    yb = ob.reshape(b, s, B_WIDTH)
    cq = rms_norm(parts[6], p['c_q_norm_g'][i]) @ p['c_w_uq'][i]
    cq = cq.reshape(b, s, C_HEADS, C_NOPE_DIM + C_ROPE_DIM)
    q_nope = cq[..., :C_NOPE_DIM]
    q_rope = apply_rope(cq[..., C_NOPE_DIM:], rope_c)
    ckv = rms_norm(parts[7], p['c_kv_norm_g'][i]) @ p['c_w_ukv'][i]
    ckv = ckv.reshape(b, s, C_HEADS, C_NOPE_DIM + C_V_DIM)
    k_nope = ckv[..., :C_NOPE_DIM]
    vc = ckv[..., C_NOPE_DIM:]
    k_rope = apply_rope(parts[8][:, :, None, :], rope_c)
    qc = jnp.concatenate([q_nope, q_rope], axis=-1)[:, :, :, None, :]
    kc = jnp.concatenate([k_nope, jnp.broadcast_to(k_rope, (b, s, C_HEADS, C_ROPE_DIM))], axis=-1)
    yc = gqa_attention(qc, kc, vc, (C_NOPE_DIM + C_ROPE_DIM) ** -0.5).reshape(b, s, C_WIDTH)
    gates = jax.nn.sigmoid(parts[9] + p['gate_b'][i]).reshape(b, s, N_BRANCH, D_MODEL)
    merged = (gates[:, :, 0] * (ya @ p['w_up_a'][i])
              + gates[:, :, 1] * (yb @ p['w_up_b'][i])
              + gates[:, :, 2] * (yc @ p['w_up_c'][i]))
    return merged @ p['w_out'][i]


def peer_ffn(h, w_q, sub_keys, u, v):
    b, s, d = h.shape
    xs = h.reshape((b * s) // PEER_CHUNK, PEER_CHUNK, d)

    def chunk(xc):
        q = (xc @ w_q).reshape(PEER_CHUNK, PEER_HEADS, 2, PEER_HALF)
        sc = jnp.einsum('chpd,phkd->chpk', q, sub_keys).astype(jnp.float32)
        s1, i1 = lax.top_k(sc[:, :, 0], PEER_TOPK)
        s2, i2 = lax.top_k(sc[:, :, 1], PEER_TOPK)
        cand = (s1[..., :, None] + s2[..., None, :]).reshape(PEER_CHUNK, PEER_HEADS, PEER_TOPK * PEER_TOPK)
        cidx = (i1[..., :, None] * PEER_N_KEYS + i2[..., None, :]).reshape(PEER_CHUNK, PEER_HEADS, PEER_TOPK * PEER_TOPK)
        top, pos = lax.top_k(cand, PEER_TOPK)
        idx = jnp.take_along_axis(cidx, pos, axis=-1)
        g = jax.nn.softmax(top, axis=-1)
        ue = jnp.take(u, idx, axis=0)
        a = jax.nn.gelu(jnp.einsum('chkd,cd->chk', ue, xc).astype(jnp.float32)) * g
        ve = jnp.take(v, idx, axis=0)
        return jnp.einsum('chk,chkd->cd', a.astype(v.dtype), ve)

    return lax.map(chunk, xs).reshape(b, s, d)


def trunk(x, p):
    s = x.shape[1]
    rows = s // GRID_W
    t = jnp.arange(s, dtype=jnp.float32)
    row_ids = jnp.repeat(jnp.arange(rows, dtype=jnp.float32), GRID_W)
    col_ids = jnp.tile(jnp.arange(GRID_W, dtype=jnp.float32), rows)
    axial_row = rope_angles(row_ids, A_AXIS_DIM, AXIAL_THETA)
    axial_col = rope_angles(col_ids, A_AXIS_DIM, AXIAL_THETA)
    rope_b = rope_angles(t, B_ROPE_DIM, ROPE_THETA)
    rope_c = rope_angles(t, C_ROPE_DIM, ROPE_THETA)
    for i in range(DEPTH):
        h = rms_norm(x, p['norm_mix_g'][i])
        x = x + token_mixing(h, p, i, axial_row, axial_col, rope_b, rope_c)
        h = rms_norm(x, p['norm_ffn_g'][i])
        x = x + peer_ffn(h, p['peer_w_q'][i], p['peer_sub_keys'][i], p['peer_u'][i], p['peer_v'][i])
    return rms_norm(x, p['final_norm_g'])


def setup_inputs(seed: int = 0) -> dict:
    key = jax.random.key(seed)
    ks = jax.random.split(key, 32)

    def nrm(k, shape, scale):
        return jax.random.normal(k, shape, dtype=jnp.float32) * scale

    def gain(k, shape):
        return 1.0 + 0.01 * jax.random.normal(k, shape, dtype=jnp.float32)

    L = DEPTH
    return {
        'x_prompt': nrm(ks[0], (BATCH, SEQ, D_MODEL), 1.0),
        'x_sample': nrm(ks[1], (DEC_BATCH, DEC_SEQ, D_MODEL), 1.0),
        'norm_mix_g': gain(ks[2], (L, D_MODEL)),
        'w_in': nrm(ks[3], (L, D_MODEL, IN_WIDTH), D_MODEL ** -0.5),
        'a_q_norm_g': gain(ks[4], (L, A_HEAD_DIM)),
        'a_k_norm_g': gain(ks[5], (L, A_HEAD_DIM)),
        'b_lambda_q1': nrm(ks[6], (L, B_HEAD_DIM), 0.1),
        'b_lambda_k1': nrm(ks[7], (L, B_HEAD_DIM), 0.1),
        'b_lambda_q2': nrm(ks[8], (L, B_HEAD_DIM), 0.1),
        'b_lambda_k2': nrm(ks[9], (L, B_HEAD_DIM), 0.1),
        'b_subln_g': gain(ks[10], (L, B_V_DIM)),
        'c_q_norm_g': gain(ks[11], (L, C_Q_RANK)),
        'c_w_uq': nrm(ks[12], (L, C_Q_RANK, C_HEADS * (C_NOPE_DIM + C_ROPE_DIM)), C_Q_RANK ** -0.5),
        'c_kv_norm_g': gain(ks[13], (L, C_KV_RANK)),
        'c_w_ukv': nrm(ks[14], (L, C_KV_RANK, C_HEADS * (C_NOPE_DIM + C_V_DIM)), C_KV_RANK ** -0.5),
        'gate_b': nrm(ks[15], (L, GATE_COLS), 0.01),
        'w_up_a': nrm(ks[16], (L, A_WIDTH, D_MODEL), A_WIDTH ** -0.5),
        'w_up_b': nrm(ks[17], (L, B_WIDTH, D_MODEL), B_WIDTH ** -0.5),
        'w_up_c': nrm(ks[18], (L, C_WIDTH, D_MODEL), C_WIDTH ** -0.5),
        'w_out': nrm(ks[19], (L, D_MODEL, D_MODEL), D_MODEL ** -0.5),
        'norm_ffn_g': gain(ks[20], (L, D_MODEL)),
        'peer_w_q': nrm(ks[21], (L, D_MODEL, PEER_HEADS * PEER_KEY_DIM), D_MODEL ** -0.5),
        'peer_sub_keys': nrm(ks[22], (L, 2, PEER_HEADS, PEER_N_KEYS, PEER_HALF), PEER_HALF ** -0.5),
        'peer_u': nrm(ks[23], (L, PEER_N_EXPERTS, D_MODEL), D_MODEL ** -0.5),
        'peer_v': nrm(ks[24], (L, PEER_N_EXPERTS, D_MODEL), 0.5 * PEER_HEADS ** -0.5),
        'final_norm_g': gain(ks[25], (D_MODEL,)),
    }


def reference(x_prompt, x_sample, norm_mix_g, w_in, a_q_norm_g, a_k_norm_g, b_lambda_q1, b_lambda_k1,
              b_lambda_q2, b_lambda_k2, b_subln_g, c_q_norm_g, c_w_uq, c_kv_norm_g, c_w_ukv, gate_b,
              w_up_a, w_up_b, w_up_c, w_out, norm_ffn_g, peer_w_q, peer_sub_keys, peer_u, peer_v,
              final_norm_g):
    p = {
        'norm_mix_g': norm_mix_g, 'w_in': w_in, 'a_q_norm_g': a_q_norm_g, 'a_k_norm_g': a_k_norm_g,
        'b_lambda_q1': b_lambda_q1, 'b_lambda_k1': b_lambda_k1, 'b_lambda_q2': b_lambda_q2,
        'b_lambda_k2': b_lambda_k2, 'b_subln_g': b_subln_g, 'c_q_norm_g': c_q_norm_g, 'c_w_uq': c_w_uq,
        'c_kv_norm_g': c_kv_norm_g, 'c_w_ukv': c_w_ukv, 'gate_b': gate_b, 'w_up_a': w_up_a,
        'w_up_b': w_up_b, 'w_up_c': w_up_c, 'w_out': w_out, 'norm_ffn_g': norm_ffn_g,
        'peer_w_q': peer_w_q, 'peer_sub_keys': peer_sub_keys, 'peer_u': peer_u, 'peer_v': peer_v,
        'final_norm_g': final_norm_g,
    }
    y_prompt = trunk(x_prompt, p)
    y_sample = trunk(x_sample, p)
    return (y_prompt, y_sample)
```

```python
import functools
import math

import jax
import jax.numpy as jnp
from jax import lax
from jax.experimental import pallas as pl
from jax.experimental.pallas import tpu as pltpu

F32 = jnp.float32
BF16 = jnp.bfloat16

D_MODEL = 1024
GRID_W = 64
RMS_EPS = 1e-6
ROPE_THETA = 500000.0
AXIAL_THETA = 10000.0

A_HEADS, A_KV_HEADS, A_HEAD_DIM = 8, 2, 64
A_AXIS_DIM = A_HEAD_DIM // 2
A_WIDTH = A_HEADS * A_HEAD_DIM
B_HEADS, B_HEAD_DIM = 4, 64
B_V_DIM = 2 * B_HEAD_DIM
B_ROPE_DIM = B_HEAD_DIM // 4
B_WIDTH = B_HEADS * B_V_DIM
C_HEADS, C_Q_RANK, C_KV_RANK = 8, 384, 256
C_NOPE_DIM, C_ROPE_DIM, C_V_DIM = 64, 32, 64
C_WIDTH = C_HEADS * C_V_DIM
N_BRANCH = 3
SPLITS = (A_HEADS * A_HEAD_DIM, A_KV_HEADS * A_HEAD_DIM, A_KV_HEADS * A_HEAD_DIM,
          B_HEADS * 2 * B_HEAD_DIM, B_HEADS * 2 * B_HEAD_DIM, B_HEADS * B_V_DIM,
          C_Q_RANK, C_KV_RANK, C_ROPE_DIM, N_BRANCH * D_MODEL)
IN_WIDTH = sum(SPLITS)
PEER_HEADS, PEER_N_KEYS, PEER_TOPK, PEER_KEY_DIM = 8, 128, 16, 256
PEER_HALF = PEER_KEY_DIM // 2
PEER_N_EXPERTS = PEER_N_KEYS * PEER_N_KEYS

V7X_VMEM_LIMIT_BYTES = 56 * 1024 * 1024
LANES = 128


def _pick_tile(n, pref):
    t = min(n, pref)
    while n % t:
        t //= 2
    return t


def _norm_matmul_kernel(x_ref, g_ref, w_ref, o_ref, h_scr, *, normalize):
    @pl.when(pl.program_id(1) == 0)
    def _():
        x = x_ref[...].astype(F32)
        if normalize:
            r = lax.rsqrt(jnp.mean(x * x, axis=-1, keepdims=True) + RMS_EPS)
            x = x * r * g_ref[...]
        h_scr[...] = x.astype(BF16)

    o_ref[...] = jnp.dot(h_scr[...], w_ref[...], preferred_element_type=F32).astype(o_ref.dtype)


def norm_matmul(x, g, w, *, normalize=True, out_dtype=F32, tm_pref=512, tn_pref=1536):
    n, k = x.shape
    nc = w.shape[1]
    tm = _pick_tile(n, tm_pref)
    tn = nc if nc <= tn_pref else tn_pref
    assert nc % tn == 0, (nc, tn)
    return pl.pallas_call(
        functools.partial(_norm_matmul_kernel, normalize=normalize),
        grid=(n // tm, nc // tn),
        in_specs=[pl.BlockSpec((tm, k), lambda i, j: (i, 0)),
                  pl.BlockSpec((1, k), lambda i, j: (0, 0)),
                  pl.BlockSpec((k, tn), lambda i, j: (0, j))],
        out_specs=pl.BlockSpec((tm, tn), lambda i, j: (i, j)),
        out_shape=jax.ShapeDtypeStruct((n, nc), out_dtype),
        scratch_shapes=[pltpu.VMEM((tm, k), BF16)],
        compiler_params=pltpu.CompilerParams(
            dimension_semantics=("arbitrary", "arbitrary"),
            vmem_limit_bytes=V7X_VMEM_LIMIT_BYTES),
        name="norm_matmul",
    )(x, g.reshape(1, k).astype(F32), w.astype(BF16))


def _flash_kernel(q_ref, k_ref, v_ref, o_ref, *, tk, nk):
    q = q_ref[0]
    tq = q.shape[0]
    dv = v_ref.shape[-1]

    def body(c, carry):
        m, l, acc = carry
        start = pl.multiple_of(c * tk, tk)
        kc = k_ref[0, pl.ds(start, tk), :]
        vc = v_ref[0, pl.ds(start, tk), :]
        s = lax.dot_general(q, kc, (((1,), (1,)), ((), ())), preferred_element_type=F32)
        m_new = jnp.maximum(m, jnp.max(s, axis=-1, keepdims=True))
        alpha = jnp.exp(m - m_new)
        p = jnp.exp(s - m_new)
        l = alpha * l + jnp.sum(p, axis=-1, keepdims=True)
        acc = alpha * acc + jnp.dot(p.astype(BF16), vc, preferred_element_type=F32)
        return m_new, l, acc

    init = (jnp.full((tq, 1), -jnp.inf, F32), jnp.zeros((tq, 1), F32), jnp.zeros((tq, dv), F32))
    _, l, acc = lax.fori_loop(0, nk, body, init)
    o_ref[0] = (acc / l).astype(o_ref.dtype)


def flash_attention(q, k, v, *, k_div, v_div, out_dtype, tq_pref=512, tk_pref=512):
    bh, s, dq = q.shape
    dv = v.shape[-1]
    tq = _pick_tile(s, tq_pref)
    tk = _pick_tile(s, tk_pref)
    return pl.pallas_call(
        functools.partial(_flash_kernel, tk=tk, nk=s // tk),
        grid=(bh, s // tq),
        in_specs=[pl.BlockSpec((1, tq, dq), lambda b, i: (b, i, 0)),
                  pl.BlockSpec((1, s, dq), lambda b, i: (b // k_div, 0, 0)),
                  pl.BlockSpec((1, s, dv), lambda b, i: (b // v_div, 0, 0))],
        out_specs=pl.BlockSpec((1, tq, dv), lambda b, i: (b, i, 0)),
        out_shape=jax.ShapeDtypeStruct((bh, s, dv), out_dtype),
        compiler_params=pltpu.CompilerParams(
            dimension_semantics=("arbitrary", "arbitrary"),
            vmem_limit_bytes=V7X_VMEM_LIMIT_BYTES),
        name="flash_attention",
    )(q, k, v)


def _merge_kernel(x_ref, ya_ref, yb_ref, yc_ref, gt_ref, gb_ref, wa_ref, wb_ref, wc_ref, wo_ref, o_ref):
    gates = jax.nn.sigmoid(gt_ref[...] + gb_ref[...])
    ua = jnp.dot(ya_ref[...], wa_ref[...], preferred_element_type=F32)
    ub = jnp.dot(yb_ref[...], wb_ref[...], preferred_element_type=F32)
    uc = jnp.dot(yc_ref[...], wc_ref[...], preferred_element_type=F32)
    merged = (gates[:, 0:D_MODEL] * ua + gates[:, D_MODEL:2 * D_MODEL] * ub
              + gates[:, 2 * D_MODEL:3 * D_MODEL] * uc)
    out = jnp.dot(merged.astype(BF16), wo_ref[...], preferred_element_type=F32)
    o_ref[...] = x_ref[...] + out


def gated_merge(x, ya, yb, yc, gates_pre, gate_b, w_up_a, w_up_b, w_up_c, w_out, *, tm_pref=512):
    n = x.shape[0]
    tm = _pick_tile(n, tm_pref)
    row = lambda w: pl.BlockSpec((tm, w), lambda i: (i, 0))
    full = lambda a: pl.BlockSpec(a.shape, lambda i: (0, 0))
    ws = [w.astype(BF16) for w in (w_up_a, w_up_b, w_up_c, w_out)]
    gb = gate_b.reshape(1, -1).astype(F32)
    return pl.pallas_call(
        _merge_kernel,
        grid=(n // tm,),
        in_specs=[row(D_MODEL), row(A_WIDTH), row(B_WIDTH), row(C_WIDTH), row(N_BRANCH * D_MODEL),
                  full(gb)] + [full(w) for w in ws],
        out_specs=row(D_MODEL),
        out_shape=jax.ShapeDtypeStruct((n, D_MODEL), F32),
        compiler_params=pltpu.CompilerParams(
            dimension_semantics=("arbitrary",),
            vmem_limit_bytes=V7X_VMEM_LIMIT_BYTES),
        name="gated_merge",
    )(x, ya, yb, yc, gates_pre, gb, *ws)


def _top_values(s, k):
    vals = []
    cur = s
    for _ in range(k):
        m = jnp.max(cur, axis=0, keepdims=True)
        vals.append(m)
        cur = jnp.where(cur == m, -jnp.inf, cur)
    return vals


def _peer_kernel(x_ref, g_ref, wq_ref, sk_ref, u_ref, vt_ref, o_ref,
                 h_scr, q_scr, s1_scr, s2_scr, e1_scr, e2_scr, tau_scr, s_scr, a_scr, acc_scr,
                 *, t, eb, lc):
    e = pl.program_id(1)
    nb = eb // PEER_N_KEYS

    @pl.when(e == 0)
    def _select():
        x = x_ref[...]
        r = lax.rsqrt(jnp.mean(x * x, axis=-1, keepdims=True) + RMS_EPS)
        h = (x * r * g_ref[...]).astype(BF16)
        h_scr[...] = h
        q_scr[...] = jnp.dot(h, wq_ref[...], preferred_element_type=F32).astype(BF16)
        acc_scr[...] = jnp.zeros_like(acc_scr)

        def head(hh, carry):
            col = pl.multiple_of(hh * PEER_KEY_DIM, PEER_KEY_DIM)
            q1 = q_scr[:, pl.ds(col, PEER_HALF)]
            q2 = q_scr[:, pl.ds(col + PEER_HALF, PEER_HALF)]
            nt = (((1,), (1,)), ((), ()))
            s1 = lax.dot_general(sk_ref[0, hh], q1, nt, preferred_element_type=F32)
            s2 = lax.dot_general(sk_ref[1, hh], q2, nt, preferred_element_type=F32)
            a = _top_values(s1, PEER_TOPK)
            b = _top_values(s2, PEER_TOPK)
            cands = [a[i] + b[j] for i in range(PEER_TOPK) for j in range(PEER_TOPK // (i + 1))]
            pad = (-len(cands)) % 8
            cand = jnp.concatenate(cands + [jnp.full((pad, t), -jnp.inf, F32)], axis=0)
            tau = _top_values(cand, PEER_TOPK)[-1]
            top = cands[0]
            z = jnp.sum(jnp.where(cand >= tau, jnp.exp(cand - top), 0.0), axis=0, keepdims=True)
            s1_scr[hh] = s1
            s2_scr[hh] = s2
            e1_scr[hh] = jnp.exp(s1 - a[0]) / z
            e2_scr[hh] = jnp.exp(s2 - b[0])
            tau_scr[hh] = jnp.broadcast_to(tau, (8, t))
            return carry

        lax.fori_loop(0, PEER_HEADS, head, 0)

    s_scr[...] = lax.dot_general(u_ref[...], h_scr[...], (((1,), (1,)), ((), ())),
                                 preferred_element_type=F32)

    def sub_block(j, carry):
        i1 = e * nb + j
        row0 = pl.multiple_of(j * PEER_N_KEYS, PEER_N_KEYS)
        for c in range(t // lc):
            cs = slice(c * lc, (c + 1) * lc)
            w = jnp.zeros((PEER_N_KEYS, lc), F32)
            for hh in range(PEER_HEADS):
                s1row = s1_scr[hh, pl.ds(i1, 1), cs]
                e1row = e1_scr[hh, pl.ds(i1, 1), cs]
                tau = tau_scr[hh, 0:1, cs]
                keep = (s1row + s2_scr[hh, :, cs]) >= tau
                w = w + jnp.where(keep, e2_scr[hh, :, cs], 0.0) * e1row
            sc = s_scr[pl.ds(row0, PEER_N_KEYS), cs]
            a_scr[pl.ds(row0, PEER_N_KEYS), cs] = (jax.nn.gelu(sc) * w).astype(BF16)
        return carry

    lax.fori_loop(0, nb, sub_block, 0)
    acc_scr[...] += jnp.dot(vt_ref[...], a_scr[...], preferred_element_type=F32)

    @pl.when(e == pl.num_programs(1) - 1)
    def _():
        o_ref[...] = x_ref[...] + acc_scr[...].T


def peer_layer(x, g, w_q, sub_keys, u, vt, *, t_pref=512, eb=1024):
    n = x.shape[0]
    t = _pick_tile(n, t_pref)
    lc = min(t, 2 * LANES)
    ne = PEER_N_EXPERTS // eb
    kd = PEER_HEADS * PEER_KEY_DIM
    sel = lambda: pltpu.VMEM((PEER_HEADS, PEER_N_KEYS, t), F32)
    return pl.pallas_call(
        functools.partial(_peer_kernel, t=t, eb=eb, lc=lc),
        grid=(n // t, ne),
        in_specs=[pl.BlockSpec((t, D_MODEL), lambda i, e: (i, 0)),
                  pl.BlockSpec((1, D_MODEL), lambda i, e: (0, 0)),
                  pl.BlockSpec((D_MODEL, kd), lambda i, e: (0, 0)),
                  pl.BlockSpec((2, PEER_HEADS, PEER_N_KEYS, PEER_HALF), lambda i, e: (0, 0, 0, 0)),
                  pl.BlockSpec((eb, D_MODEL), lambda i, e: (e, 0)),
                  pl.BlockSpec((D_MODEL, eb), lambda i, e: (0, e))],
        out_specs=pl.BlockSpec((t, D_MODEL), lambda i, e: (i, 0)),
        out_shape=jax.ShapeDtypeStruct((n, D_MODEL), F32),
        scratch_shapes=[pltpu.VMEM((t, D_MODEL), BF16),
                        pltpu.VMEM((t, kd), BF16),
                        sel(), sel(), sel(), sel(),
                        pltpu.VMEM((PEER_HEADS, 8, t), F32),
                        pltpu.VMEM((eb, t), F32),
                        pltpu.VMEM((eb, t), BF16),
                        pltpu.VMEM((D_MODEL, t), F32)],
        compiler_params=pltpu.CompilerParams(
            dimension_semantics=("arbitrary", "arbitrary"),
            vmem_limit_bytes=V7X_VMEM_LIMIT_BYTES),
        name="peer",
    )(x, g.reshape(1, D_MODEL).astype(F32), w_q.astype(BF16), sub_keys.astype(BF16), u, vt)


def _rmsnorm_kernel(x_ref, g_ref, o_ref):
    x = x_ref[...]
    r = lax.rsqrt(jnp.mean(x * x, axis=-1, keepdims=True) + RMS_EPS)
    o_ref[...] = x * r * g_ref[...]


def rmsnorm_rows(x, g, *, tm_pref=1024):
    n, d = x.shape
    tm = _pick_tile(n, tm_pref)
    return pl.pallas_call(
        _rmsnorm_kernel,
        grid=(n // tm,),
        in_specs=[pl.BlockSpec((tm, d), lambda i: (i, 0)), pl.BlockSpec((1, d), lambda i: (0, 0))],
        out_specs=pl.BlockSpec((tm, d), lambda i: (i, 0)),
        out_shape=jax.ShapeDtypeStruct((n, d), F32),
        name="final_rmsnorm",
    )(x, g.reshape(1, d).astype(F32))


def _rms_norm(x, g):
    y = x * lax.rsqrt(jnp.mean(x * x, axis=-1, keepdims=True) + RMS_EPS)
    return y * g


def _rope_angles(pos, dim, theta):
    inv = 1.0 / (theta ** (jnp.arange(0, dim, 2, dtype=F32) / dim))
    ang = pos[:, None] * inv[None, :]
    return jnp.cos(ang), jnp.sin(ang)


def _apply_rope(x, cs):
    cos, sin = cs
    extra = x.ndim - 3
    shp = (cos.shape[0],) + (1,) * extra + (cos.shape[1],)
    cos = cos.reshape(shp)
    sin = sin.reshape(shp)
    half = x.shape[-1] // 2
    x1, x2 = x[..., :half], x[..., half:]
    return jnp.concatenate([x1 * cos - x2 * sin, x2 * cos + x1 * sin], axis=-1)


def _head_major(x):
    b, s, h, d = x.shape
    return jnp.transpose(x, (0, 2, 1, 3)).reshape(b * h, s, d)


def _token_major(x, b):
    bh, s, d = x.shape
    h = bh // b
    return jnp.transpose(x.reshape(b, h, s, d), (0, 2, 1, 3)).reshape(b, s, h * d)


def _mix_group(proj, cq_all, ckv_all, p, i, b, s, lam, lam_init):
    rows = s // GRID_W
    t = jnp.arange(s, dtype=F32)
    row_ids = jnp.repeat(jnp.arange(rows, dtype=F32), GRID_W)
    col_ids = jnp.tile(jnp.arange(GRID_W, dtype=F32), rows)
    axial_row = _rope_angles(row_ids, A_AXIS_DIM, AXIAL_THETA)
    axial_col = _rope_angles(col_ids, A_AXIS_DIM, AXIAL_THETA)
    rope_b = _rope_angles(t, B_ROPE_DIM, ROPE_THETA)
    rope_c = _rope_angles(t, C_ROPE_DIM, ROPE_THETA)

    proj = proj.reshape(b, s, -1)
    parts = []
    off = 0
    for w in SPLITS:
        parts.append(proj[..., off:off + w])
        off += w

    qa = _rms_norm(parts[0].reshape(b, s, A_HEADS, A_HEAD_DIM), p['a_q_norm_g'][i])
    ka = _rms_norm(parts[1].reshape(b, s, A_KV_HEADS, A_HEAD_DIM), p['a_k_norm_g'][i])
    va = parts[2].reshape(b, s, A_KV_HEADS, A_HEAD_DIM)
    axial = lambda x: jnp.concatenate([_apply_rope(x[..., :A_AXIS_DIM], axial_row),
                                       _apply_rope(x[..., A_AXIS_DIM:], axial_col)], axis=-1)
    qa = axial(qa) * (A_HEAD_DIM ** -0.5)
    ka = axial(ka)
    ya = flash_attention(_head_major(qa).astype(BF16), _head_major(ka).astype(BF16),
                         _head_major(va).astype(BF16), k_div=A_HEADS // A_KV_HEADS,
                         v_div=A_HEADS // A_KV_HEADS, out_dtype=BF16)
    ya = _token_major(ya, b)

    qb = parts[3].reshape(b, s, B_HEADS * 2, B_HEAD_DIM)
    kb = parts[4].reshape(b, s, B_HEADS * 2, B_HEAD_DIM)
    vb = parts[5].reshape(b, s, B_HEADS, B_V_DIM)
    prope = lambda x: jnp.concatenate([_apply_rope(x[..., :B_ROPE_DIM], rope_b), x[..., B_ROPE_DIM:]], axis=-1)
    qb = prope(qb) * (B_HEAD_DIM ** -0.5)
    kb = prope(kb)
    ob = flash_attention(_head_major(qb).astype(BF16), _head_major(kb).astype(BF16),
                         _head_major(vb).astype(BF16), k_div=1, v_div=2, out_dtype=F32)
    ob = ob.reshape(b, B_HEADS, 2, s, B_V_DIM)
    ob = ob[:, :, 0] - lam * ob[:, :, 1]
    ob = _rms_norm(ob, p['b_subln_g'][i]) * (1.0 - lam_init)
    yb = jnp.transpose(ob, (0, 2, 1, 3)).reshape(b, s, B_WIDTH).astype(BF16)

    cq = cq_all.reshape(b, s, C_HEADS, C_NOPE_DIM + C_ROPE_DIM)
    q_rope = _apply_rope(cq[..., C_NOPE_DIM:], rope_c)
    ckv = ckv_all.reshape(b, s, C_HEADS, C_NOPE_DIM + C_V_DIM)
    k_nope = ckv[..., :C_NOPE_DIM]
    vc = ckv[..., C_NOPE_DIM:]
    k_rope = _apply_rope(parts[8][:, :, None, :], rope_c)
    zpad = jnp.zeros((b, s, C_HEADS, LANES - C_NOPE_DIM - C_ROPE_DIM), F32)
    qc = jnp.concatenate([cq[..., :C_NOPE_DIM], q_rope, zpad], axis=-1) * ((C_NOPE_DIM + C_ROPE_DIM) ** -0.5)
    kc = jnp.concatenate([k_nope, jnp.broadcast_to(k_rope, (b, s, C_HEADS, C_ROPE_DIM)), zpad], axis=-1)
    yc = flash_attention(_head_major(qc).astype(BF16), _head_major(kc).astype(BF16),
                         _head_major(vc).astype(BF16), k_div=1, v_div=1, out_dtype=BF16)
    yc = _token_major(yc, b)

    n = b * s
    return ya.reshape(n, A_WIDTH), yb.reshape(n, B_WIDTH), yc.reshape(n, C_WIDTH)


def _forward(groups, p, depth):
    shapes = [(g.shape[0], g.shape[1]) for g in groups]
    sizes = [b * s for b, s in shapes]
    offs = [0]
    for n in sizes:
        offs.append(offs[-1] + n)
    x = jnp.concatenate([g.reshape(-1, D_MODEL) for g in groups], axis=0)
    gate_off = IN_WIDTH - N_BRANCH * D_MODEL
    in_pad = (-IN_WIDTH) % 1536 if IN_WIDTH > 1536 else 0

    for i in range(depth):
        w_in = jnp.pad(p['w_in'][i], ((0, 0), (0, in_pad)))
        proj = norm_matmul(x, p['norm_mix_g'][i], w_in)
        off_cq = sum(SPLITS[:6])
        off_ckv = off_cq + C_Q_RANK
        cq_all = norm_matmul(proj[:, off_cq:off_cq + C_Q_RANK], p['c_q_norm_g'][i], p['c_w_uq'][i])
        ckv_all = norm_matmul(proj[:, off_ckv:off_ckv + C_KV_RANK], p['c_kv_norm_g'][i], p['c_w_ukv'][i])

        lam_init = 0.8 - 0.6 * math.exp(-0.3 * i)
        lam = (jnp.exp(jnp.sum(p['b_lambda_q1'][i] * p['b_lambda_k1'][i]))
               - jnp.exp(jnp.sum(p['b_lambda_q2'][i] * p['b_lambda_k2'][i])) + lam_init)
        ys = [_mix_group(proj[offs[gi]:offs[gi + 1]], cq_all[offs[gi]:offs[gi + 1]],
                         ckv_all[offs[gi]:offs[gi + 1]], p, i, b, s, lam, lam_init)
              for gi, (b, s) in enumerate(shapes)]
        ya, yb, yc = (jnp.concatenate([y[k] for y in ys], axis=0) for k in range(3))
        x = gated_merge(x, ya, yb, yc, proj[:, gate_off:gate_off + N_BRANCH * D_MODEL], p['gate_b'][i],
                        p['w_up_a'][i], p['w_up_b'][i], p['w_up_c'][i], p['w_out'][i])

        u = p['peer_u'][i].astype(BF16)
        vt = p['peer_v'][i].astype(BF16).T
        x = peer_layer(x, p['norm_ffn_g'][i], p['peer_w_q'][i], p['peer_sub_keys'][i], u, vt)

    y = rmsnorm_rows(x, p['final_norm_g'])
    return [y[offs[gi]:offs[gi + 1]].reshape(b, s, D_MODEL) for gi, (b, s) in enumerate(shapes)]


def kernel(x_prompt, x_sample, norm_mix_g, w_in, a_q_norm_g, a_k_norm_g, b_lambda_q1, b_lambda_k1, b_lambda_q2, b_lambda_k2, b_subln_g, c_q_norm_g, c_w_uq, c_kv_norm_g, c_w_ukv, gate_b, w_up_a, w_up_b, w_up_c, w_out, norm_ffn_g, peer_w_q, peer_sub_keys, peer_u, peer_v, final_norm_g):
    p = {
        'norm_mix_g': norm_mix_g, 'w_in': w_in, 'a_q_norm_g': a_q_norm_g, 'a_k_norm_g': a_k_norm_g,
        'b_lambda_q1': b_lambda_q1, 'b_lambda_k1': b_lambda_k1, 'b_lambda_q2': b_lambda_q2,
        'b_lambda_k2': b_lambda_k2, 'b_subln_g': b_subln_g, 'c_q_norm_g': c_q_norm_g, 'c_w_uq': c_w_uq,
        'c_kv_norm_g': c_kv_norm_g, 'c_w_ukv': c_w_ukv, 'gate_b': gate_b, 'w_up_a': w_up_a,
        'w_up_b': w_up_b, 'w_up_c': w_up_c, 'w_out': w_out, 'norm_ffn_g': norm_ffn_g,
        'peer_w_q': peer_w_q, 'peer_sub_keys': peer_sub_keys, 'peer_u': peer_u, 'peer_v': peer_v,
        'final_norm_g': final_norm_g,
    }
    y_prompt, y_sample = _forward([x_prompt, x_sample], p, w_in.shape[0])
    return (y_prompt, y_sample)
```

```python
import functools
import math

import jax
import jax.numpy as jnp
import numpy as np
from jax import lax
from jax.experimental import pallas as pl
from jax.experimental.pallas import tpu as pltpu

F32 = jnp.float32
BF16 = jnp.bfloat16

D_MODEL = 1024
GRID_W = 64
RMS_EPS = 1e-6
ROPE_THETA = 500000.0
AXIAL_THETA = 10000.0

A_HEADS, A_KV_HEADS, A_HEAD_DIM = 8, 2, 64
A_AXIS_DIM = A_HEAD_DIM // 2
A_WIDTH = A_HEADS * A_HEAD_DIM
B_HEADS, B_HEAD_DIM = 4, 64
B_V_DIM = 2 * B_HEAD_DIM
B_ROPE_DIM = B_HEAD_DIM // 4
B_WIDTH = B_HEADS * B_V_DIM
C_HEADS, C_Q_RANK, C_KV_RANK = 8, 384, 256
C_NOPE_DIM, C_ROPE_DIM, C_V_DIM = 64, 32, 64
C_WIDTH = C_HEADS * C_V_DIM
N_BRANCH = 3
GATE_COLS = N_BRANCH * D_MODEL
SPLITS = (A_HEADS * A_HEAD_DIM, A_KV_HEADS * A_HEAD_DIM, A_KV_HEADS * A_HEAD_DIM,
          B_HEADS * 2 * B_HEAD_DIM, B_HEADS * 2 * B_HEAD_DIM, B_HEADS * B_V_DIM,
          C_Q_RANK, C_KV_RANK, C_ROPE_DIM, GATE_COLS)
IN_WIDTH = sum(SPLITS)
PEER_HEADS, PEER_N_KEYS, PEER_TOPK, PEER_KEY_DIM = 8, 128, 16, 256
PEER_HALF = PEER_KEY_DIM // 2
PEER_N_EXPERTS = PEER_N_KEYS * PEER_N_KEYS

V7X_VMEM_LIMIT_BYTES = 58 * 1024 * 1024
LANES = 128
LOG2E = math.log2(math.e)

_SEC_WIDTHS = (('qa', 512), ('qa_sw', 512), ('ka', 256), ('ka_sw', 256), ('va', 128),
               ('qb', 512), ('qb_sw', 512), ('kb', 512), ('kb_sw', 512), ('vb', 512),
               ('cql', C_Q_RANK), ('ckvl', C_KV_RANK), ('kr', 128), ('kr_sw', 128), ('gates', GATE_COLS))
_SEC = {}
_off = 0
for _name, _w in _SEC_WIDTHS:
    _SEC[_name] = (_off, _off + _w)
    _off += _w
W_BIG_COLS = _off


def _pick_tile(n, pref):
    t = min(n, pref)
    while n % t:
        t //= 2
    return t


def _swap_halves(n_groups, group, rope_dims, half):
    idx = np.arange(n_groups * group)
    for g in range(n_groups):
        for start in rope_dims:
            lo = g * group + start
            idx[lo:lo + half] = np.arange(lo + half, lo + 2 * half)
            idx[lo + half:lo + 2 * half] = np.arange(lo, lo + half)
    return idx


_SW_A = _swap_halves(1, 64, (0, 32), 16)
_SW_B = _swap_halves(1, 64, (0,), 8)
_SW_C = _swap_halves(1, 32, (0,), 16)


def _prep_w_in(w):
    offs = np.cumsum((0,) + SPLITS)
    aq, ak, av, bq, bk, bv, cq, ckv, kr, gt = (np.arange(offs[i], offs[i + 1]) for i in range(10))
    sw_q = np.concatenate([64 * h + _SW_A for h in range(A_HEADS)])
    sw_b = np.concatenate([64 * h + _SW_B for h in range(2 * B_HEADS)])
    ka_rep = np.concatenate([np.arange(0, 64), np.arange(0, 64), np.arange(64, 128), np.arange(64, 128)])
    ka_rep_sw = np.concatenate([_SW_A, _SW_A, 64 + _SW_A, 64 + _SW_A])
    zeros = lambda n: jnp.zeros((w.shape[0], n), w.dtype)
    kr_blk = jnp.concatenate([zeros(64), w[:, kr], zeros(32)], axis=1)
    kr_blk_sw = jnp.concatenate([zeros(64), w[:, kr[_SW_C]], zeros(32)], axis=1)
    cols = [w[:, aq], w[:, aq[sw_q]], w[:, ak[ka_rep]], w[:, ak[ka_rep_sw]], w[:, av],
            w[:, bq], w[:, bq[sw_b]], w[:, bk], w[:, bk[sw_b]], w[:, bv],
            w[:, cq], w[:, ckv], kr_blk, kr_blk_sw, w[:, gt]]
    return jnp.concatenate(cols, axis=1).astype(BF16)


def _prep_w_uq(w):
    r = w.shape[0]
    w3 = w.reshape(r, C_HEADS, C_NOPE_DIM + C_ROPE_DIM)
    z32 = jnp.zeros((r, C_HEADS, 32), w.dtype)
    z64 = jnp.zeros((r, C_HEADS, 64), w.dtype)
    main = jnp.concatenate([w3, z32], axis=-1)
    part = jnp.concatenate([z64, w3[:, :, C_NOPE_DIM + _SW_C], z32], axis=-1)
    return jnp.concatenate([main.reshape(r, -1), part.reshape(r, -1)], axis=1).astype(BF16)


def _prep_w_ukv(w):
    r = w.shape[0]
    w3 = w.reshape(r, C_HEADS, C_NOPE_DIM + C_V_DIM)
    kn = jnp.concatenate([w3[:, :, :C_NOPE_DIM], jnp.zeros((r, C_HEADS, 64), w.dtype)], axis=-1)
    return jnp.concatenate([kn.reshape(r, -1), w3[:, :, C_NOPE_DIM:].reshape(r, -1)], axis=1).astype(BF16)


def _rope_tables(s_max):
    def angles(pos, dim, theta):
        inv = 1.0 / (theta ** (jnp.arange(0, dim, 2, dtype=F32) / dim))
        ang = pos[:, None] * inv[None, :]
        return jnp.cos(ang), jnp.sin(ang)

    t = jnp.arange(s_max, dtype=F32)
    row = jnp.floor(t / GRID_W)
    col = t - row * GRID_W
    cr, sr = angles(row, A_AXIS_DIM, AXIAL_THETA)
    cc, sc = angles(col, A_AXIS_DIM, AXIAL_THETA)
    cos_a = jnp.concatenate([cr, cr, cc, cc] * 2, axis=1)
    sin_a = jnp.concatenate([-sr, sr, -sc, sc] * 2, axis=1)
    cb, sb = angles(t, B_ROPE_DIM, ROPE_THETA)
    one = lambda n: jnp.ones((s_max, n), F32)
    zero = lambda n: jnp.zeros((s_max, n), F32)
    cos_b = jnp.concatenate([cb, cb, one(48)] * 2, axis=1)
    sin_b = jnp.concatenate([-sb, sb, zero(48)] * 2, axis=1)
    c3, s3 = angles(t, C_ROPE_DIM, ROPE_THETA)
    cos_c = jnp.concatenate([one(64), c3, c3, one(32)], axis=1)
    sin_c = jnp.concatenate([zero(64), -s3, s3, zero(32)], axis=1)
    return jnp.concatenate([cos_a, sin_a, cos_b, sin_b, cos_c, sin_c], axis=1)


def _proj_kernel(x_ref, g_ref, w_ref, tab_ref, hg_ref, bd_ref, gcq_ref, wuq_ref, gckv_ref, wukv_ref, gb_ref,
                 qa_ref, ka_ref, va_ref, qb_ref, kb_ref, vb_ref, qc_ref, kc_ref, vc_ref, gt_ref):
    x = x_ref[...]
    r = lax.rsqrt(jnp.mean(x * x, axis=-1, keepdims=True) + RMS_EPS)
    h = (x * r * g_ref[...]).astype(BF16)
    tm = x.shape[0]

    def mm(name):
        lo, hi = _SEC[name]
        return jnp.dot(h, w_ref[:, lo:hi], preferred_element_type=F32)

    tab = lambda k: tab_ref[:, k * LANES:(k + 1) * LANES]
    cos_a, sin_a, cos_b, sin_b, cos_c, sin_c = (tab(k) for k in range(6))
    hg = lambda k: hg_ref[k:k + 1, :]
    gq, gq_sw, gk, gk_sw = hg(0), hg(1), hg(2), hg(3)
    blk = lambda a, j: a[:, j * LANES:(j + 1) * LANES]
    lane = lax.broadcasted_iota(jnp.int32, (tm, LANES), 1)
    half_mask = (lane < 64, lane >= 64)

    def head_rms(v):
        w = v.shape[1]
        ss = jnp.dot((v * v).astype(BF16), bd_ref[0:w, 0:w], preferred_element_type=F32)
        return lax.rsqrt(ss * (1.0 / A_HEAD_DIM) + RMS_EPS)

    qa, qa_sw = mm('qa'), mm('qa_sw')
    rq = head_rms(qa)
    for j in range(4):
        v = (blk(qa, j) * gq * cos_a + blk(qa_sw, j) * gq_sw * sin_a) * blk(rq, j) * (A_HEAD_DIM ** -0.5 * LOG2E)
        for c in range(2):
            hd = 2 * j + c
            qa_ref[:, hd * LANES:(hd + 1) * LANES] = jnp.where(half_mask[c], v, 0.0).astype(BF16)
    ka, ka_sw = mm('ka'), mm('ka_sw')
    rk = head_rms(ka)
    for j in range(2):
        v = (blk(ka, j) * gk * cos_a + blk(ka_sw, j) * gk_sw * sin_a) * blk(rk, j)
        ka_ref[:, j * LANES:(j + 1) * LANES] = v.astype(BF16)
    va_ref[...] = mm('va').astype(BF16)

    qb, qb_sw = mm('qb'), mm('qb_sw')
    for j in range(4):
        v = (blk(qb, j) * cos_b + blk(qb_sw, j) * sin_b) * (B_HEAD_DIM ** -0.5 * LOG2E)
        for c in range(2):
            hd = 2 * j + c
            qb_ref[:, hd * LANES:(hd + 1) * LANES] = jnp.where(half_mask[c], v, 0.0).astype(BF16)
    kb, kb_sw = mm('kb'), mm('kb_sw')
    for j in range(4):
        kb_ref[:, j * LANES:(j + 1) * LANES] = (blk(kb, j) * cos_b + blk(kb_sw, j) * sin_b).astype(BF16)
    vb_ref[...] = mm('vb').astype(BF16)

    def latent(name, g):
        v = mm(name)
        rr = lax.rsqrt(jnp.mean(v * v, axis=-1, keepdims=True) + RMS_EPS)
        return (v * rr * g).astype(BF16)

    cq2 = jnp.dot(latent('cql', gcq_ref[...]), wuq_ref[...], preferred_element_type=F32)
    nq = C_HEADS * LANES
    for j in range(C_HEADS):
        v = (blk(cq2, j) * cos_c + blk(cq2, C_HEADS + j) * sin_c) * ((C_NOPE_DIM + C_ROPE_DIM) ** -0.5 * LOG2E)
        qc_ref[:, j * LANES:(j + 1) * LANES] = v.astype(BF16)
    ckv2 = jnp.dot(latent('ckvl', gckv_ref[...]), wukv_ref[...], preferred_element_type=F32)
    k_rope = mm('kr') * cos_c + mm('kr_sw') * sin_c
    for j in range(C_HEADS):
        kc_ref[:, j * LANES:(j + 1) * LANES] = (blk(ckv2, j) + k_rope).astype(BF16)
    vc_ref[...] = ckv2[:, nq:nq + C_WIDTH].astype(BF16)

    gt_ref[...] = jax.nn.sigmoid(mm('gates') + gb_ref[...]).astype(BF16)


def _pos_tile_map(groups, tm):
    def index_map(i):
        pos = jnp.int32(0)
        for off, _, s in groups:
            local = i - off // tm
            pos = jnp.where(local >= 0, local % (s // tm), pos)
        return pos, 0
    return index_map


def fused_projection(x, groups, g, w_big, tables, head_g, blockdiag, g_cq, w_uq, g_ckv, w_ukv, gate_b, *, tm):
    n = x.shape[0]
    const = lambda a: pl.BlockSpec(a.shape, lambda i: (0,) * a.ndim, pipeline_mode=pl.Buffered(1))
    row = lambda w: pl.BlockSpec((tm, w), lambda i: (i, 0))
    out_w = (1024, 256, 128, 1024, 512, 512, 1024, 1024, 512, GATE_COLS)
    g2 = g.reshape(1, -1).astype(F32)
    gcq2 = g_cq.reshape(1, -1).astype(F32)
    gckv2 = g_ckv.reshape(1, -1).astype(F32)
    gb2 = gate_b.reshape(1, -1).astype(F32)
    return pl.pallas_call(
        _proj_kernel,
        grid=(n // tm,),
        in_specs=[row(D_MODEL), const(g2), const(w_big),
                  pl.BlockSpec((tm, 6 * LANES), _pos_tile_map(groups, tm)),
                  const(head_g), const(blockdiag), const(gcq2), const(w_uq), const(gckv2), const(w_ukv), const(gb2)],
        out_specs=[row(w) for w in out_w],
        out_shape=[jax.ShapeDtypeStruct((n, w), BF16) for w in out_w],
        compiler_params=pltpu.CompilerParams(
            dimension_semantics=("arbitrary",),
            vmem_limit_bytes=V7X_VMEM_LIMIT_BYTES),
        name="fused_projection",
    )(x, g2, w_big, tables, head_g, blockdiag, gcq2, w_uq, gckv2, w_ukv, gb2)


def _flash_kernel(q_ref, k_ref, v_ref, o_ref, *, tk, nk, dv, row_stride, row_mod):
    q = q_ref[...]
    tq = q.shape[0]
    nt = (((1,), (1,)), ((), ()))
    tn = (((0,), (0,)), ((), ()))

    def body(c, carry):
        m, l, acc = carry
        start = pl.multiple_of(c * tk, tk)
        kc = k_ref[pl.ds(start, tk), :]
        vc = v_ref[pl.ds(start, tk), :]
        s = lax.dot_general(kc, q, nt, preferred_element_type=F32)
        m_new = jnp.maximum(m, jnp.max(s, axis=0, keepdims=True))
        alpha = jnp.exp2(m - m_new)
        p = jnp.exp2(s - m_new)
        l = alpha * l + jnp.sum(p, axis=0, keepdims=True)
        acc = alpha * acc + lax.dot_general(vc, p.astype(BF16), tn, preferred_element_type=F32)
        return m_new, l, acc

    init = (jnp.full((1, tq), -jnp.inf, F32), jnp.zeros((1, tq), F32), jnp.zeros((LANES, tq), F32))
    _, l, acc = lax.fori_loop(0, nk, body, init, unroll=True)
    out = acc * (1.0 / l)
    if dv == LANES:
        o_ref[...] = out.astype(o_ref.dtype)
    else:
        hd = pl.program_id(1)
        sel = (hd // row_stride) % row_mod
        o_ref[...] = jnp.where(sel == 0, out[0:dv], out[dv:2 * dv]).astype(o_ref.dtype)


def flash_attention(q, k, v, group, *, n_heads, k_div, v_div, dv, row_stride, row_mod, out_dtype,
                    tq_pref=512, tk_pref=512):
    off, b, s = group
    tq = _pick_tile(s, tq_pref)
    tk = _pick_tile(s, tk_pref)
    assert off % s == 0
    qb0, kb0, nq = off // tq, off // s, s // tq
    return pl.pallas_call(
        functools.partial(_flash_kernel, tk=tk, nk=s // tk, dv=dv, row_stride=row_stride, row_mod=row_mod),
        grid=(b, n_heads, nq),
        in_specs=[pl.BlockSpec((tq, LANES), lambda bi, h, i: (qb0 + bi * nq + i, h)),
                  pl.BlockSpec((s, LANES), lambda bi, h, i: (kb0 + bi, h // k_div)),
                  pl.BlockSpec((s, LANES), lambda bi, h, i: (kb0 + bi, h // v_div))],
        out_specs=pl.BlockSpec((dv, tq), lambda bi, h, i: (h, bi * nq + i)),
        out_shape=jax.ShapeDtypeStruct((n_heads * dv, b * s), out_dtype),
        compiler_params=pltpu.CompilerParams(
            dimension_semantics=("arbitrary", "arbitrary", "arbitrary"),
            vmem_limit_bytes=V7X_VMEM_LIMIT_BYTES),
        name="flash_attention",
    )(q, k, v)


def _merge_kernel(sc_ref, x_ref, ya_ref, ob_ref, yc_ref, gt_ref, gs_ref, wa_ref, wb_ref, wc_ref, wo_ref, o_ref):
    lam = sc_ref[0]
    post = sc_ref[1]
    tn = (((0,), (0,)), ((), ()))
    ybs = []
    for hd in range(B_HEADS):
        o0 = ob_ref[(2 * hd) * B_V_DIM:(2 * hd + 1) * B_V_DIM, :]
        o1 = ob_ref[(2 * hd + 1) * B_V_DIM:(2 * hd + 2) * B_V_DIM, :]
        d = o0 - lam * o1
        rr = lax.rsqrt(jnp.mean(d * d, axis=0, keepdims=True) + RMS_EPS)
        ybs.append(((d * rr * gs_ref[...]) * post).astype(BF16))
    yb = jnp.concatenate(ybs, axis=0)
    ua = lax.dot_general(ya_ref[...], wa_ref[...], tn, preferred_element_type=F32)
    ub = lax.dot_general(yb, wb_ref[...], tn, preferred_element_type=F32)
    uc = lax.dot_general(yc_ref[...], wc_ref[...], tn, preferred_element_type=F32)
    gt = gt_ref[...].astype(F32)
    merged = (gt[:, 0:D_MODEL] * ua + gt[:, D_MODEL:2 * D_MODEL] * ub + gt[:, 2 * D_MODEL:3 * D_MODEL] * uc)
    o_ref[...] = x_ref[...] + jnp.dot(merged.astype(BF16), wo_ref[...], preferred_element_type=F32)


def gated_merge(scalars, x, ya_t, ob_t, yc_t, gates, subln_g, w_up_a, w_up_b, w_up_c, w_out, *, tm):
    n = x.shape[0]
    row = lambda w: pl.BlockSpec((tm, w), lambda i: (i, 0))
    col = lambda r: pl.BlockSpec((r, tm), lambda i: (0, i))
    const = lambda a: pl.BlockSpec(a.shape, lambda i: (0,) * a.ndim, pipeline_mode=pl.Buffered(1))
    ws = [w.astype(BF16) for w in (w_up_a, w_up_b, w_up_c, w_out)]
    gs = subln_g.reshape(-1, 1).astype(F32)
    return pl.pallas_call(
        _merge_kernel,
        grid=(n // tm,),
        in_specs=[pl.BlockSpec(memory_space=pltpu.SMEM),
                  row(D_MODEL), col(A_WIDTH), col(2 * B_WIDTH), col(C_WIDTH), row(GATE_COLS), const(gs)]
                 + [const(w) for w in ws],
        out_specs=row(D_MODEL),
        out_shape=jax.ShapeDtypeStruct((n, D_MODEL), F32),
        compiler_params=pltpu.CompilerParams(
            dimension_semantics=("arbitrary",),
            vmem_limit_bytes=V7X_VMEM_LIMIT_BYTES),
        name="gated_merge",
    )(scalars, x, ya_t, ob_t, yc_t, gates, gs, *ws)


SUBLANES = 8


def _oddeven_merge_sort_pairs(n):
    def merge(lo, hi, r):
        step = r * 2
        if step < hi - lo:
            yield from merge(lo, hi, step)
            yield from merge(lo + r, hi, step)
            yield from [(i, i + r) for i in range(lo + r, hi - r, step)]
        else:
            yield (lo, lo + r)

    def sort(lo, hi):
        if hi - lo >= 1:
            mid = lo + (hi - lo) // 2
            yield from sort(lo, mid)
            yield from sort(mid + 1, hi)
            yield from merge(lo, hi, 1)

    return list(sort(0, n - 1))


def _compare_exchange(xs, pairs):
    xs = list(xs)
    for i, j in pairs:
        hi, lo = jnp.maximum(xs[i], xs[j]), jnp.minimum(xs[i], xs[j])
        xs[i], xs[j] = hi, lo
    return xs


def _bitonic_merge16(xs):
    pairs = [(i, i + d) for d in (8, 4, 2, 1) for i in range(PEER_TOPK) if not i & d]
    return _compare_exchange(xs, pairs)


def _top16_columns(groups):
    xs = _compare_exchange(groups, _oddeven_merge_sort_pairs(len(groups)))
    for shift in (4, 2, 1):
        ys = [pltpu.roll(x, shift, 0) for x in xs]
        if len(xs) == 8:
            zs = xs + ys[::-1]
        else:
            zs = [jnp.maximum(xs[k], ys[PEER_TOPK - 1 - k]) for k in range(PEER_TOPK)]
        xs = _bitonic_merge16(zs)
    return xs


def _sublane_sum(x):
    for shift in (4, 2, 1):
        x = x + pltpu.roll(x, shift, 0)
    return x


def _peer_select_chunk(s1, s2):
    w = s1.shape[1]
    rows = lambda s: [s[SUBLANES * g:SUBLANES * (g + 1), :] for g in range(PEER_N_KEYS // SUBLANES)]
    g1, g2 = rows(s1), rows(s2)
    a = _top16_columns(g1)
    b = _top16_columns(g2)
    cands = [a[i] + b[j] for i in range(PEER_TOPK) for j in range(PEER_TOPK // (i + 1))]
    sub = lax.broadcasted_iota(jnp.int32, (SUBLANES, w), 0)
    neg = jnp.full((SUBLANES, w), -jnp.inf, F32)
    packed = []
    for g in range(8):
        v = neg
        for r in range(SUBLANES):
            k = SUBLANES * g + r
            if k < len(cands):
                v = jnp.where(sub == r, cands[k], v)
        packed.append(v)
    tau = _top16_columns(packed)[PEER_TOPK - 1]
    z = sum(jnp.where(pk >= tau, jnp.exp(pk - cands[0]), 0.0) for pk in packed)
    zinv = 1.0 / _sublane_sum(z)
    e1 = jnp.concatenate([jnp.exp(g - a[0]) * zinv for g in g1], axis=0)
    e2 = jnp.concatenate([jnp.exp(g - b[0]) for g in g2], axis=0)
    return tau, e1, e2


def _peer_kernel(x_ref, g_ref, wq_ref, sk_ref, u_ref, vt_ref, o_ref,
                 h_scr, q_scr, s1_scr, s2_scr, e1_scr, e2_scr, tau_scr, s_scr, a_scr, acc_scr,
                 *, t, eb, lc, ne):
    e = pl.program_id(1)
    nb = eb // PEER_N_KEYS

    @pl.when(e == 0)
    def _select():
        x = x_ref[...]
        r = lax.rsqrt(jnp.mean(x * x, axis=-1, keepdims=True) + RMS_EPS)
        h = (x * r * g_ref[...]).astype(BF16)
        h_scr[...] = h
        q_scr[...] = jnp.dot(h, wq_ref[...], preferred_element_type=F32).astype(BF16)
        acc_scr[...] = jnp.zeros_like(acc_scr)
        a_scr[1] = jnp.zeros((eb, t), BF16)

        def head(hh, carry):
            col = pl.multiple_of(hh * PEER_KEY_DIM, PEER_KEY_DIM)
            q1 = q_scr[:, pl.ds(col, PEER_HALF)]
            q2 = q_scr[:, pl.ds(col + PEER_HALF, PEER_HALF)]
            nt = (((1,), (1,)), ((), ()))
            s1_scr[hh] = lax.dot_general(sk_ref[0, hh], q1, nt, preferred_element_type=F32)
            s2_scr[hh] = lax.dot_general(sk_ref[1, hh], q2, nt, preferred_element_type=F32)
            for c in range(t // LANES):
                cs = slice(c * LANES, (c + 1) * LANES)
                tau, e1, e2 = _peer_select_chunk(s1_scr[hh, :, cs], s2_scr[hh, :, cs])
                tau_scr[hh, :, cs] = tau
                e1_scr[hh, :, cs] = e1
                e2_scr[hh, :, cs] = e2
            return carry

        lax.fori_loop(0, PEER_HEADS, head, 0)

    def value_matmul():
        acc_scr[...] += jnp.dot(vt_ref[...], a_scr[(e + 1) % 2], preferred_element_type=F32)

    @pl.when(e < ne)
    def _experts():
        s_scr[...] = lax.dot_general(u_ref[...], h_scr[...], (((1,), (1,)), ((), ())),
                                     preferred_element_type=F32)
        slot = e % 2
        for j in range(nb):
            i1 = e * nb + j
            rs = slice(j * PEER_N_KEYS, (j + 1) * PEER_N_KEYS)
            for c in range(t // lc):
                cs = slice(c * lc, (c + 1) * lc)
                w = jnp.zeros((PEER_N_KEYS, lc), F32)
                for hh in range(PEER_HEADS):
                    s1row = s1_scr[hh, pl.ds(i1, 1), cs]
                    e1row = e1_scr[hh, pl.ds(i1, 1), cs]
                    tau = tau_scr[hh, 0:1, cs]
                    keep = (s1row + s2_scr[hh, :, cs]) >= tau
                    w = w + jnp.where(keep, e2_scr[hh, :, cs], 0.0) * e1row
                a_scr[slot, rs, cs] = (jax.nn.gelu(s_scr[rs, cs]) * w).astype(BF16)
        value_matmul()

    @pl.when(e == ne)
    def _finish():
        value_matmul()
        o_ref[...] = x_ref[...] + acc_scr[...].T


def peer_layer(x, g, w_q, sub_keys, u, vt, *, t_pref=512, eb=1024):
    n = x.shape[0]
    t = _pick_tile(n, t_pref)
    lc = min(t, 2 * LANES)
    ne = PEER_N_EXPERTS // eb
    kd = PEER_HEADS * PEER_KEY_DIM
    sel = lambda: pltpu.VMEM((PEER_HEADS, PEER_N_KEYS, t), F32)
    const = lambda shape: pl.BlockSpec(shape, lambda i, e: (0,) * len(shape), pipeline_mode=pl.Buffered(1))
    return pl.pallas_call(
        functools.partial(_peer_kernel, t=t, eb=eb, lc=lc, ne=ne),
        grid=(n // t, ne + 1),
        in_specs=[pl.BlockSpec((t, D_MODEL), lambda i, e: (i, 0)),
                  const((1, D_MODEL)),
                  const((D_MODEL, kd)),
                  const((2, PEER_HEADS, PEER_N_KEYS, PEER_HALF)),
                  pl.BlockSpec((eb, D_MODEL), lambda i, e: (jnp.minimum(e, ne - 1), 0)),
                  pl.BlockSpec((D_MODEL, eb), lambda i, e: (0, jnp.maximum(e - 1, 0)))],
        out_specs=pl.BlockSpec((t, D_MODEL), lambda i, e: (i, 0)),
        out_shape=jax.ShapeDtypeStruct((n, D_MODEL), F32),
        scratch_shapes=[pltpu.VMEM((t, D_MODEL), BF16),
                        pltpu.VMEM((t, kd), BF16),
                        sel(), sel(), sel(), sel(),
                        pltpu.VMEM((PEER_HEADS, SUBLANES, t), F32),
                        pltpu.VMEM((eb, t), F32),
                        pltpu.VMEM((2, eb, t), BF16),
                        pltpu.VMEM((D_MODEL, t), F32)],
        compiler_params=pltpu.CompilerParams(
            dimension_semantics=("arbitrary", "arbitrary"),
            vmem_limit_bytes=V7X_VMEM_LIMIT_BYTES),
        name="peer",
    )(x, g.reshape(1, D_MODEL).astype(F32), w_q.astype(BF16), sub_keys.astype(BF16), u, vt)


def _rmsnorm_kernel(x_ref, g_ref, o_ref):
    x = x_ref[...]
    r = lax.rsqrt(jnp.mean(x * x, axis=-1, keepdims=True) + RMS_EPS)
    o_ref[...] = x * r * g_ref[...]


def rmsnorm_rows(x, g, *, tm_pref=1024):
    n, d = x.shape
    tm = _pick_tile(n, tm_pref)
    return pl.pallas_call(
        _rmsnorm_kernel,
        grid=(n // tm,),
        in_specs=[pl.BlockSpec((tm, d), lambda i: (i, 0)), pl.BlockSpec((1, d), lambda i: (0, 0))],
        out_specs=pl.BlockSpec((tm, d), lambda i: (i, 0)),
        out_shape=jax.ShapeDtypeStruct((n, d), F32),
        name="final_rmsnorm",
    )(x, g.reshape(1, d).astype(F32))


def _forward(xs, p, depth):
    groups = []
    off = 0
    for g in xs:
        groups.append((off, g.shape[0], g.shape[1]))
        off += g.shape[0] * g.shape[1]
    x = jnp.concatenate([g.reshape(-1, D_MODEL) for g in xs], axis=0)
    tm = 512
    for _, _, s in groups:
        tm = math.gcd(tm, s)
    tables = _rope_tables(max(s for _, _, s in groups))
    ones64 = np.kron(np.eye(8, dtype=np.float32), np.ones((64, 64), np.float32))
    blockdiag = jnp.asarray(ones64, BF16)

    for i in range(depth):
        gq, gk = p['a_q_norm_g'][i].astype(F32), p['a_k_norm_g'][i].astype(F32)
        head_g = jnp.stack([jnp.tile(gq, 2), jnp.tile(gq[_SW_A], 2), jnp.tile(gk, 2), jnp.tile(gk[_SW_A], 2)]
                           + [jnp.zeros((LANES,), F32)] * 4)
        qa, ka, va, qb, kb, vb, qc, kc, vc, gates = fused_projection(
            x, groups, p['norm_mix_g'][i], _prep_w_in(p['w_in'][i]), tables, head_g, blockdiag,
            p['c_q_norm_g'][i], _prep_w_uq(p['c_w_uq'][i]), p['c_kv_norm_g'][i], _prep_w_ukv(p['c_w_ukv'][i]),
            p['gate_b'][i], tm=tm)

        ya, ob, yc = [], [], []
        for grp in groups:
            ya.append(flash_attention(qa, ka, va, grp, n_heads=A_HEADS, k_div=4, v_div=A_HEADS, dv=64,
                                      row_stride=4, row_mod=2, out_dtype=BF16))
            ob.append(flash_attention(qb, kb, vb, grp, n_heads=2 * B_HEADS, k_div=2, v_div=2, dv=LANES,
                                      row_stride=1, row_mod=1, out_dtype=F32))
            yc.append(flash_attention(qc, kc, vc, grp, n_heads=C_HEADS, k_div=1, v_div=2, dv=64,
                                      row_stride=1, row_mod=2, out_dtype=BF16))
        cat = lambda parts: parts[0] if len(parts) == 1 else jnp.concatenate(parts, axis=1)

        lam_init = 0.8 - 0.6 * math.exp(-0.3 * i)
        f32 = lambda a: a.astype(F32)
        lam = (jnp.exp(jnp.sum(f32(p['b_lambda_q1'][i]) * f32(p['b_lambda_k1'][i])))
               - jnp.exp(jnp.sum(f32(p['b_lambda_q2'][i]) * f32(p['b_lambda_k2'][i]))) + lam_init)
        scalars = jnp.stack([lam, jnp.float32(1.0 - lam_init)]).astype(F32)
        x = gated_merge(scalars, x, cat(ya), cat(ob), cat(yc), gates, p['b_subln_g'][i],
                        p['w_up_a'][i], p['w_up_b'][i], p['w_up_c'][i], p['w_out'][i], tm=tm)

        u = p['peer_u'][i].astype(BF16)
        vt = p['peer_v'][i].astype(BF16).T
        x = peer_layer(x, p['norm_ffn_g'][i], p['peer_w_q'][i], p['peer_sub_keys'][i], u, vt)

    y = rmsnorm_rows(x, p['final_norm_g'])
    return [y[off:off + b * s].reshape(b, s, D_MODEL) for off, b, s in groups]


def kernel(x_prompt, x_sample, norm_mix_g, w_in, a_q_norm_g, a_k_norm_g, b_lambda_q1, b_lambda_k1, b_lambda_q2, b_lambda_k2, b_subln_g, c_q_norm_g, c_w_uq, c_kv_norm_g, c_w_ukv, gate_b, w_up_a, w_up_b, w_up_c, w_out, norm_ffn_g, peer_w_q, peer_sub_keys, peer_u, peer_v, final_norm_g):
    p = {
        'norm_mix_g': norm_mix_g, 'w_in': w_in, 'a_q_norm_g': a_q_norm_g, 'a_k_norm_g': a_k_norm_g,
        'b_lambda_q1': b_lambda_q1, 'b_lambda_k1': b_lambda_k1, 'b_lambda_q2': b_lambda_q2,
        'b_lambda_k2': b_lambda_k2, 'b_subln_g': b_subln_g, 'c_q_norm_g': c_q_norm_g, 'c_w_uq': c_w_uq,
        'c_kv_norm_g': c_kv_norm_g, 'c_w_ukv': c_w_ukv, 'gate_b': gate_b, 'w_up_a': w_up_a,
        'w_up_b': w_up_b, 'w_up_c': w_up_c, 'w_out': w_out, 'norm_ffn_g': norm_ffn_g,
        'peer_w_q': peer_w_q, 'peer_sub_keys': peer_sub_keys, 'peer_u': peer_u, 'peer_v': peer_v,
        'final_norm_g': final_norm_g,
    }
    y_prompt, y_sample = _forward([x_prompt, x_sample], p, w_in.shape[0])
    return (y_prompt, y_sample)
```

```python
import functools
import math

import jax
import jax.numpy as jnp
import numpy as np
from jax import lax
from jax.experimental import pallas as pl
from jax.experimental.pallas import tpu as pltpu

F32 = jnp.float32
BF16 = jnp.bfloat16

D_MODEL = 1024
GRID_W = 64
RMS_EPS = 1e-6
ROPE_THETA = 500000.0
AXIAL_THETA = 10000.0

A_HEADS, A_KV_HEADS, A_HEAD_DIM = 8, 2, 64
A_AXIS_DIM = A_HEAD_DIM // 2
A_WIDTH = A_HEADS * A_HEAD_DIM
B_HEADS, B_HEAD_DIM = 4, 64
B_V_DIM = 2 * B_HEAD_DIM
B_ROPE_DIM = B_HEAD_DIM // 4
B_WIDTH = B_HEADS * B_V_DIM
C_HEADS, C_Q_RANK, C_KV_RANK = 8, 384, 256
C_NOPE_DIM, C_ROPE_DIM, C_V_DIM = 64, 32, 64
C_WIDTH = C_HEADS * C_V_DIM
N_BRANCH = 3
GATE_COLS = N_BRANCH * D_MODEL
SPLITS = (A_HEADS * A_HEAD_DIM, A_KV_HEADS * A_HEAD_DIM, A_KV_HEADS * A_HEAD_DIM,
          B_HEADS * 2 * B_HEAD_DIM, B_HEADS * 2 * B_HEAD_DIM, B_HEADS * B_V_DIM,
          C_Q_RANK, C_KV_RANK, C_ROPE_DIM, GATE_COLS)
IN_WIDTH = sum(SPLITS)
PEER_HEADS, PEER_N_KEYS, PEER_TOPK, PEER_KEY_DIM = 8, 128, 16, 256
PEER_HALF = PEER_KEY_DIM // 2
PEER_N_EXPERTS = PEER_N_KEYS * PEER_N_KEYS

V7X_VMEM_LIMIT_BYTES = 58 * 1024 * 1024
LANES = 128
LOG2E = math.log2(math.e)

_SEC_WIDTHS = (('qa', 512), ('qa_sw', 512), ('ka', 256), ('ka_sw', 256), ('va', 128),
               ('qb', 512), ('qb_sw', 512), ('kb', 512), ('kb_sw', 512), ('vb', 512),
               ('cql', C_Q_RANK), ('ckvl', C_KV_RANK), ('kr', 128), ('kr_sw', 128), ('gates', GATE_COLS))
_SEC = {}
_off = 0
for _name, _w in _SEC_WIDTHS:
    _SEC[_name] = (_off, _off + _w)
    _off += _w
W_BIG_COLS = _off


def _pick_tile(n, pref):
    t = min(n, pref)
    while n % t:
        t //= 2
    return t


def _swap_halves(n_groups, group, rope_dims, half):
    idx = np.arange(n_groups * group)
    for g in range(n_groups):
        for start in rope_dims:
            lo = g * group + start
            idx[lo:lo + half] = np.arange(lo + half, lo + 2 * half)
            idx[lo + half:lo + 2 * half] = np.arange(lo, lo + half)
    return idx


_SW_A = _swap_halves(1, 64, (0, 32), 16)
_SW_B = _swap_halves(1, 64, (0,), 8)
_SW_C = _swap_halves(1, 32, (0,), 16)


def _prep_w_in(w):
    offs = np.cumsum((0,) + SPLITS)
    aq, ak, av, bq, bk, bv, cq, ckv, kr, gt = (np.arange(offs[i], offs[i + 1]) for i in range(10))
    sw_q = np.concatenate([64 * h + _SW_A for h in range(A_HEADS)])
    sw_b = np.concatenate([64 * h + _SW_B for h in range(2 * B_HEADS)])
    ka_rep = np.concatenate([np.arange(0, 64), np.arange(0, 64), np.arange(64, 128), np.arange(64, 128)])
    ka_rep_sw = np.concatenate([_SW_A, _SW_A, 64 + _SW_A, 64 + _SW_A])
    zeros = lambda n: jnp.zeros((w.shape[0], n), w.dtype)
    kr_blk = jnp.concatenate([zeros(64), w[:, kr], zeros(32)], axis=1)
    kr_blk_sw = jnp.concatenate([zeros(64), w[:, kr[_SW_C]], zeros(32)], axis=1)
    cols = [w[:, aq], w[:, aq[sw_q]], w[:, ak[ka_rep]], w[:, ak[ka_rep_sw]], w[:, av],
            w[:, bq], w[:, bq[sw_b]], w[:, bk], w[:, bk[sw_b]], w[:, bv],
            w[:, cq], w[:, ckv], kr_blk, kr_blk_sw, w[:, gt]]
    return jnp.concatenate(cols, axis=1).astype(BF16)


def _prep_w_uq(w):
    r = w.shape[0]
    w3 = w.reshape(r, C_HEADS, C_NOPE_DIM + C_ROPE_DIM)
    z32 = jnp.zeros((r, C_HEADS, 32), w.dtype)
    z64 = jnp.zeros((r, C_HEADS, 64), w.dtype)
    main = jnp.concatenate([w3, z32], axis=-1)
    part = jnp.concatenate([z64, w3[:, :, C_NOPE_DIM + _SW_C], z32], axis=-1)
    return jnp.concatenate([main.reshape(r, -1), part.reshape(r, -1)], axis=1).astype(BF16)


def _prep_w_ukv(w):
    r = w.shape[0]
    w3 = w.reshape(r, C_HEADS, C_NOPE_DIM + C_V_DIM)
    kn = jnp.concatenate([w3[:, :, :C_NOPE_DIM], jnp.zeros((r, C_HEADS, 64), w.dtype)], axis=-1)
    return jnp.concatenate([kn.reshape(r, -1), w3[:, :, C_NOPE_DIM:].reshape(r, -1)], axis=1).astype(BF16)


def _rope_tables(s_max):
    def angles(pos, dim, theta):
        inv = 1.0 / (theta ** (jnp.arange(0, dim, 2, dtype=F32) / dim))
        ang = pos[:, None] * inv[None, :]
        return jnp.cos(ang), jnp.sin(ang)

    t = jnp.arange(s_max, dtype=F32)
    row = jnp.floor(t / GRID_W)
    col = t - row * GRID_W
    cr, sr = angles(row, A_AXIS_DIM, AXIAL_THETA)
    cc, sc = angles(col, A_AXIS_DIM, AXIAL_THETA)
    cos_a = jnp.concatenate([cr, cr, cc, cc] * 2, axis=1)
    sin_a = jnp.concatenate([-sr, sr, -sc, sc] * 2, axis=1)
    cb, sb = angles(t, B_ROPE_DIM, ROPE_THETA)
    one = lambda n: jnp.ones((s_max, n), F32)
    zero = lambda n: jnp.zeros((s_max, n), F32)
    cos_b = jnp.concatenate([cb, cb, one(48)] * 2, axis=1)
    sin_b = jnp.concatenate([-sb, sb, zero(48)] * 2, axis=1)
    c3, s3 = angles(t, C_ROPE_DIM, ROPE_THETA)
    cos_c = jnp.concatenate([one(64), c3, c3, one(32)], axis=1)
    sin_c = jnp.concatenate([zero(64), -s3, s3, zero(32)], axis=1)
    return jnp.concatenate([cos_a, sin_a, cos_b, sin_b, cos_c, sin_c], axis=1)


def _proj_kernel(x_ref, g_ref, w_ref, tab_ref, hg_ref, bd_ref, gcq_ref, wuq_ref, gckv_ref, wukv_ref, gb_ref,
                 qa_ref, ka_ref, va_ref, qb_ref, kb_ref, vb_ref, qc_ref, kc_ref, vc_ref, gt_ref):
    x = x_ref[...]
    r = lax.rsqrt(jnp.mean(x * x, axis=-1, keepdims=True) + RMS_EPS)
    h = (x * r * g_ref[...]).astype(BF16)
    tm = x.shape[0]

    def mm(name):
        lo, hi = _SEC[name]
        return jnp.dot(h, w_ref[:, lo:hi], preferred_element_type=F32)

    tab = lambda k: tab_ref[:, k * LANES:(k + 1) * LANES]
    cos_a, sin_a, cos_b, sin_b, cos_c, sin_c = (tab(k) for k in range(6))
    hg = lambda k: hg_ref[k:k + 1, :]
    gq, gq_sw, gk, gk_sw = hg(0), hg(1), hg(2), hg(3)
    blk = lambda a, j: a[:, j * LANES:(j + 1) * LANES]
    lane = lax.broadcasted_iota(jnp.int32, (tm, LANES), 1)
    half_mask = (lane < 64, lane >= 64)

    def head_rms(v):
        w = v.shape[1]
        ss = jnp.dot((v * v).astype(BF16), bd_ref[0:w, 0:w], preferred_element_type=F32)
        return lax.rsqrt(ss * (1.0 / A_HEAD_DIM) + RMS_EPS)

    qa, qa_sw = mm('qa'), mm('qa_sw')
    rq = head_rms(qa)
    for j in range(4):
        v = (blk(qa, j) * gq * cos_a + blk(qa_sw, j) * gq_sw * sin_a) * blk(rq, j) * (A_HEAD_DIM ** -0.5 * LOG2E)
        for c in range(2):
            hd = 2 * j + c
            qa_ref[:, hd * LANES:(hd + 1) * LANES] = jnp.where(half_mask[c], v, 0.0).astype(BF16)
    ka, ka_sw = mm('ka'), mm('ka_sw')
    rk = head_rms(ka)
    for j in range(2):
        v = (blk(ka, j) * gk * cos_a + blk(ka_sw, j) * gk_sw * sin_a) * blk(rk, j)
        ka_ref[:, j * LANES:(j + 1) * LANES] = v.astype(BF16)
    va_ref[...] = mm('va').astype(BF16)

    qb, qb_sw = mm('qb'), mm('qb_sw')
    for j in range(4):
        v = (blk(qb, j) * cos_b + blk(qb_sw, j) * sin_b) * (B_HEAD_DIM ** -0.5 * LOG2E)
        for c in range(2):
            hd = 2 * j + c
            qb_ref[:, hd * LANES:(hd + 1) * LANES] = jnp.where(half_mask[c], v, 0.0).astype(BF16)
    kb, kb_sw = mm('kb'), mm('kb_sw')
    for j in range(4):
        kb_ref[:, j * LANES:(j + 1) * LANES] = (blk(kb, j) * cos_b + blk(kb_sw, j) * sin_b).astype(BF16)
    vb_ref[...] = mm('vb').astype(BF16)

    def latent(name, g):
        v = mm(name)
        rr = lax.rsqrt(jnp.mean(v * v, axis=-1, keepdims=True) + RMS_EPS)
        return (v * rr * g).astype(BF16)

    cq2 = jnp.dot(latent('cql', gcq_ref[...]), wuq_ref[...], preferred_element_type=F32)
    nq = C_HEADS * LANES
    for j in range(C_HEADS):
        v = (blk(cq2, j) * cos_c + blk(cq2, C_HEADS + j) * sin_c) * ((C_NOPE_DIM + C_ROPE_DIM) ** -0.5 * LOG2E)
        qc_ref[:, j * LANES:(j + 1) * LANES] = v.astype(BF16)
    ckv2 = jnp.dot(latent('ckvl', gckv_ref[...]), wukv_ref[...], preferred_element_type=F32)
    k_rope = mm('kr') * cos_c + mm('kr_sw') * sin_c
    for j in range(C_HEADS):
        kc_ref[:, j * LANES:(j + 1) * LANES] = (blk(ckv2, j) + k_rope).astype(BF16)
    vc_ref[...] = ckv2[:, nq:nq + C_WIDTH].astype(BF16)

    gt_ref[...] = jax.nn.sigmoid(mm('gates') + gb_ref[...]).astype(BF16)


def _pos_tile_map(groups, tm):
    def index_map(i):
        pos = jnp.int32(0)
        for off, _, s in groups:
            local = i - off // tm
            pos = jnp.where(local >= 0, local % (s // tm), pos)
        return pos, 0
    return index_map


def fused_projection(x, groups, g, w_big, tables, head_g, blockdiag, g_cq, w_uq, g_ckv, w_ukv, gate_b, *, tm):
    n = x.shape[0]
    const = lambda a: pl.BlockSpec(a.shape, lambda i: (0,) * a.ndim, pipeline_mode=pl.Buffered(1))
    row = lambda w: pl.BlockSpec((tm, w), lambda i: (i, 0))
    out_w = (1024, 256, 128, 1024, 512, 512, 1024, 1024, 512, GATE_COLS)
    g2 = g.reshape(1, -1).astype(F32)
    gcq2 = g_cq.reshape(1, -1).astype(F32)
    gckv2 = g_ckv.reshape(1, -1).astype(F32)
    gb2 = gate_b.reshape(1, -1).astype(F32)
    return pl.pallas_call(
        _proj_kernel,
        grid=(n // tm,),
        in_specs=[row(D_MODEL), const(g2), const(w_big),
                  pl.BlockSpec((tm, 6 * LANES), _pos_tile_map(groups, tm)),
                  const(head_g), const(blockdiag), const(gcq2), const(w_uq), const(gckv2), const(w_ukv), const(gb2)],
        out_specs=[row(w) for w in out_w],
        out_shape=[jax.ShapeDtypeStruct((n, w), BF16) for w in out_w],
        compiler_params=pltpu.CompilerParams(
            dimension_semantics=("arbitrary",),
            vmem_limit_bytes=V7X_VMEM_LIMIT_BYTES),
        name="fused_projection",
    )(x, g2, w_big, tables, head_g, blockdiag, gcq2, w_uq, gckv2, w_ukv, gb2)


def _flash_kernel(q_ref, k_ref, v_ref, o_ref, *, tk, nk, dv, row_stride, row_mod):
    q = q_ref[...]
    tq = q.shape[0]
    nt = (((1,), (1,)), ((), ()))
    tn = (((0,), (0,)), ((), ()))

    def body(c, carry):
        m, l, acc = carry
        start = pl.multiple_of(c * tk, tk)
        kc = k_ref[pl.ds(start, tk), :]
        vc = v_ref[pl.ds(start, tk), :]
        s = lax.dot_general(kc, q, nt, preferred_element_type=F32)
        m_new = jnp.maximum(m, jnp.max(s, axis=0, keepdims=True))
        alpha = jnp.exp2(m - m_new)
        p = jnp.exp2(s - m_new)
        l = alpha * l + jnp.sum(p, axis=0, keepdims=True)
        acc = alpha * acc + lax.dot_general(vc, p.astype(BF16), tn, preferred_element_type=F32)
        return m_new, l, acc

    init = (jnp.full((1, tq), -jnp.inf, F32), jnp.zeros((1, tq), F32), jnp.zeros((LANES, tq), F32))
    _, l, acc = lax.fori_loop(0, nk, body, init, unroll=True)
    out = acc * (1.0 / l)
    if dv == LANES:
        o_ref[...] = out.astype(o_ref.dtype)
    else:
        hd = pl.program_id(1)
        sel = (hd // row_stride) % row_mod
        o_ref[...] = jnp.where(sel == 0, out[0:dv], out[dv:2 * dv]).astype(o_ref.dtype)


def flash_attention(q, k, v, group, *, n_heads, k_div, v_div, dv, row_stride, row_mod, out_dtype,
                    tq_pref=2048, tk_pref=512):
    off, b, s = group
    tq = _pick_tile(s, tq_pref)
    tk = _pick_tile(s, tk_pref)
    assert off % s == 0
    qb0, kb0, nq = off // tq, off // s, s // tq
    return pl.pallas_call(
        functools.partial(_flash_kernel, tk=tk, nk=s // tk, dv=dv, row_stride=row_stride, row_mod=row_mod),
        grid=(b, n_heads, nq),
        in_specs=[pl.BlockSpec((tq, LANES), lambda bi, h, i: (qb0 + bi * nq + i, h)),
                  pl.BlockSpec((s, LANES), lambda bi, h, i: (kb0 + bi, h // k_div)),
                  pl.BlockSpec((s, LANES), lambda bi, h, i: (kb0 + bi, h // v_div))],
        out_specs=pl.BlockSpec((dv, tq), lambda bi, h, i: (h, bi * nq + i)),
        out_shape=jax.ShapeDtypeStruct((n_heads * dv, b * s), out_dtype),
        compiler_params=pltpu.CompilerParams(
            dimension_semantics=("arbitrary", "arbitrary", "arbitrary"),
            vmem_limit_bytes=V7X_VMEM_LIMIT_BYTES),
        name="flash_attention",
    )(q, k, v)


def _merge_kernel(sc_ref, x_ref, ya_ref, ob_ref, yc_ref, gt_ref, gs_ref, wa_ref, wb_ref, wc_ref, wo_ref, o_ref):
    lam = sc_ref[0]
    post = sc_ref[1]
    tn = (((0,), (0,)), ((), ()))
    ybs = []
    for hd in range(B_HEADS):
        o0 = ob_ref[(2 * hd) * B_V_DIM:(2 * hd + 1) * B_V_DIM, :]
        o1 = ob_ref[(2 * hd + 1) * B_V_DIM:(2 * hd + 2) * B_V_DIM, :]
        d = o0 - lam * o1
        rr = lax.rsqrt(jnp.mean(d * d, axis=0, keepdims=True) + RMS_EPS)
        ybs.append(((d * rr * gs_ref[...]) * post).astype(BF16))
    yb = jnp.concatenate(ybs, axis=0)
    ua = lax.dot_general(ya_ref[...], wa_ref[...], tn, preferred_element_type=F32)
    ub = lax.dot_general(yb, wb_ref[...], tn, preferred_element_type=F32)
    uc = lax.dot_general(yc_ref[...], wc_ref[...], tn, preferred_element_type=F32)
    gt = gt_ref[...].astype(F32)
    merged = (gt[:, 0:D_MODEL] * ua + gt[:, D_MODEL:2 * D_MODEL] * ub + gt[:, 2 * D_MODEL:3 * D_MODEL] * uc)
    o_ref[...] = x_ref[...] + jnp.dot(merged.astype(BF16), wo_ref[...], preferred_element_type=F32)


def gated_merge(scalars, x, ya_t, ob_t, yc_t, gates, subln_g, w_up_a, w_up_b, w_up_c, w_out, *, tm):
    n = x.shape[0]
    row = lambda w: pl.BlockSpec((tm, w), lambda i: (i, 0))
    col = lambda r: pl.BlockSpec((r, tm), lambda i: (0, i))
    const = lambda a: pl.BlockSpec(a.shape, lambda i: (0,) * a.ndim, pipeline_mode=pl.Buffered(1))
    ws = [w.astype(BF16) for w in (w_up_a, w_up_b, w_up_c, w_out)]
    gs = subln_g.reshape(-1, 1).astype(F32)
    return pl.pallas_call(
        _merge_kernel,
        grid=(n // tm,),
        in_specs=[pl.BlockSpec(memory_space=pltpu.SMEM),
                  row(D_MODEL), col(A_WIDTH), col(2 * B_WIDTH), col(C_WIDTH), row(GATE_COLS), const(gs)]
                 + [const(w) for w in ws],
        out_specs=row(D_MODEL),
        out_shape=jax.ShapeDtypeStruct((n, D_MODEL), F32),
        compiler_params=pltpu.CompilerParams(
            dimension_semantics=("arbitrary",),
            vmem_limit_bytes=V7X_VMEM_LIMIT_BYTES),
        name="gated_merge",
    )(scalars, x, ya_t, ob_t, yc_t, gates, gs, *ws)


SUBLANES = 8


def _oddeven_merge_sort_pairs(n):
    def merge(lo, hi, r):
        step = r * 2
        if step < hi - lo:
            yield from merge(lo, hi, step)
            yield from merge(lo + r, hi, step)
            yield from [(i, i + r) for i in range(lo + r, hi - r, step)]
        else:
            yield (lo, lo + r)

    def sort(lo, hi):
        if hi - lo >= 1:
            mid = lo + (hi - lo) // 2
            yield from sort(lo, mid)
            yield from sort(mid + 1, hi)
            yield from merge(lo, hi, 1)

    return list(sort(0, n - 1))


def _compare_exchange(xs, pairs):
    xs = list(xs)
    for i, j in pairs:
        hi, lo = jnp.maximum(xs[i], xs[j]), jnp.minimum(xs[i], xs[j])
        xs[i], xs[j] = hi, lo
    return xs


def _bitonic_merge16(xs):
    pairs = [(i, i + d) for d in (8, 4, 2, 1) for i in range(PEER_TOPK) if not i & d]
    return _compare_exchange(xs, pairs)


def _top16_columns(groups):
    xs = _compare_exchange(groups, _oddeven_merge_sort_pairs(len(groups)))
    for shift in (4, 2, 1):
        ys = [pltpu.roll(x, shift, 0) for x in xs]
        if len(xs) == 8:
            zs = xs + ys[::-1]
        else:
            zs = [jnp.maximum(xs[k], ys[PEER_TOPK - 1 - k]) for k in range(PEER_TOPK)]
        xs = _bitonic_merge16(zs)
    return xs


def _sublane_sum(x):
    for shift in (4, 2, 1):
        x = x + pltpu.roll(x, shift, 0)
    return x


def _peer_select_chunk(s1, s2):
    w = s1.shape[1]
    rows = lambda s: [s[SUBLANES * g:SUBLANES * (g + 1), :] for g in range(PEER_N_KEYS // SUBLANES)]
    g1, g2 = rows(s1), rows(s2)
    a = _top16_columns(g1)
    b = _top16_columns(g2)
    cand_rows = [[a[i] + b[j] for j in range(PEER_TOPK // (i + 1))] for i in range(PEER_TOPK)]
    cands = [c for row in cand_rows for c in row]
    sub = lax.broadcasted_iota(jnp.int32, (SUBLANES, w), 0)
    neg = jnp.full((SUBLANES, w), -jnp.inf, F32)
    packed = []
    for g in range(8):
        v = neg
        for r in range(SUBLANES):
            k = SUBLANES * g + r
            if k < len(cands):
                v = jnp.where(sub == r, cands[k], v)
        packed.append(v)
    tau = _top16_columns(packed)[PEER_TOPK - 1]
    z = sum(jnp.where(pk >= tau, jnp.exp(pk - cands[0]), 0.0) for pk in packed)
    zinv = 1.0 / _sublane_sum(z)
    partners = [sum(jnp.where(c >= tau, 1.0, 0.0) for c in row) for row in cand_rows]

    def lookup(g, keys, vals, default):
        v = jnp.full((SUBLANES, w), default, F32)
        for key, val in zip(keys, vals):
            v = jnp.where(g == key, val, v)
        return v

    c1 = jnp.concatenate([lookup(g, a, partners, 0.0) for g in g1], axis=0)
    r2 = jnp.concatenate([lookup(g, b, [float(k) for k in range(PEER_TOPK)], float(PEER_TOPK)) for g in g2], axis=0)
    e1 = jnp.concatenate([jnp.exp(g - a[0]) * zinv for g in g1], axis=0)
    e2 = jnp.concatenate([jnp.exp(g - b[0]) for g in g2], axis=0)
    return c1, e1, r2.astype(BF16), e2.astype(BF16)


def _peer_kernel(x_ref, g_ref, wq_ref, sk_ref, u_ref, vt_ref, o_ref,
                 h_scr, q_scr, s1_scr, s2_scr, c1_scr, e1_scr, r2_scr, e2_scr, c1cur_scr, e1cur_scr,
                 s_scr, sprev_scr, acc_scr,
                 *, t, eb, lc, ne):
    e = pl.program_id(1)
    nb = eb // PEER_N_KEYS
    bf16_rows = 2 * SUBLANES
    pv_rows = 2

    def key_row(ref, hh, j, cs):
        row = jnp.broadcast_to(ref[hh, j:j + 1, cs], (bf16_rows, lc)).astype(BF16)
        return jnp.concatenate([row] * (PEER_N_KEYS // bf16_rows), axis=0)

    @pl.when(e == 0)
    def _select():
        x = x_ref[...]
        r = lax.rsqrt(jnp.mean(x * x, axis=-1, keepdims=True) + RMS_EPS)
        h = (x * r * g_ref[...]).astype(BF16)
        h_scr[...] = h
        q_scr[...] = jnp.dot(h, wq_ref[...], preferred_element_type=F32).astype(BF16)
        acc_scr[...] = jnp.zeros_like(acc_scr)

        def head(hh, carry):
            col = pl.multiple_of(hh * PEER_KEY_DIM, PEER_KEY_DIM)
            q1 = q_scr[:, pl.ds(col, PEER_HALF)]
            q2 = q_scr[:, pl.ds(col + PEER_HALF, PEER_HALF)]
            nt = (((1,), (1,)), ((), ()))
            s1_scr[...] = lax.dot_general(sk_ref[0, hh], q1, nt, preferred_element_type=F32)
            s2_scr[...] = lax.dot_general(sk_ref[1, hh], q2, nt, preferred_element_type=F32)
            for c in range(t // LANES):
                cs = slice(c * LANES, (c + 1) * LANES)
                c1, e1, r2, e2 = _peer_select_chunk(s1_scr[:, cs], s2_scr[:, cs])
                c1_scr[hh, :, cs] = c1
                e1_scr[hh, :, cs] = e1
                r2_scr[hh, :, cs] = r2
                e2_scr[hh, :, cs] = e2
            return carry

        lax.fori_loop(0, PEER_HEADS, head, 0)

    def score_matmul():
        s_scr[...] = lax.dot_general(u_ref[...], h_scr[...], (((1,), (1,)), ((), ())),
                                     preferred_element_type=F32)

    def stage_previous_block():
        row0 = pl.multiple_of((e - 1) * nb, nb)
        c1cur_scr[...] = c1_scr[:, pl.ds(row0, nb), :]
        e1cur_scr[...] = e1_scr[:, pl.ds(row0, nb), :]
        sprev_scr[...] = s_scr[...]

    def gate_and_accumulate():
        s_prev = sprev_scr
        for part in range(nb // pv_rows):
            acts = []
            for j in range(part * pv_rows, (part + 1) * pv_rows):
                rs = slice(j * PEER_N_KEYS, (j + 1) * PEER_N_KEYS)
                cols = []
                for c in range(t // lc):
                    cs = slice(c * lc, (c + 1) * lc)
                    w = jnp.zeros((PEER_N_KEYS, lc), BF16)
                    for hh in range(PEER_HEADS):
                        c1row = key_row(c1cur_scr, hh, j, cs)
                        e1row = key_row(e1cur_scr, hh, j, cs)
                        keep = r2_scr[hh, :, cs] < c1row
                        w = w + jnp.where(keep, e2_scr[hh, :, cs], 0) * e1row
                    cols.append(jax.nn.gelu(s_prev[rs, cs]).astype(BF16) * w)
                acts.append(jnp.concatenate(cols, axis=1))
            act = jnp.concatenate(acts, axis=0)
            ks = slice(part * pv_rows * PEER_N_KEYS, (part + 1) * pv_rows * PEER_N_KEYS)
            acc_scr[...] += jnp.dot(vt_ref[:, ks], act, preferred_element_type=F32)

    @pl.when(e == 0)
    def _first():
        score_matmul()

    @pl.when((e > 0) & (e < ne))
    def _steady():
        stage_previous_block()
        score_matmul()
        gate_and_accumulate()

    @pl.when(e == ne)
    def _finish():
        stage_previous_block()
        gate_and_accumulate()
        o_ref[...] = x_ref[...] + acc_scr[...].T


def peer_layer(x, g, w_q, sub_keys, u, vt, *, t_pref=512, eb=1024):
    n = x.shape[0]
    t = _pick_tile(n, t_pref)
    lc = min(t, 2 * LANES)
    ne = PEER_N_EXPERTS // eb
    kd = PEER_HEADS * PEER_KEY_DIM
    sel = lambda dt: pltpu.VMEM((PEER_HEADS, PEER_N_KEYS, t), dt)
    const = lambda shape: pl.BlockSpec(shape, lambda i, e: (0,) * len(shape), pipeline_mode=pl.Buffered(1))
    return pl.pallas_call(
        functools.partial(_peer_kernel, t=t, eb=eb, lc=lc, ne=ne),
        grid=(n // t, ne + 1),
        in_specs=[pl.BlockSpec((t, D_MODEL), lambda i, e: (i, 0)),
                  const((1, D_MODEL)),
                  const((D_MODEL, kd)),
                  const((2, PEER_HEADS, PEER_N_KEYS, PEER_HALF)),
                  pl.BlockSpec((eb, D_MODEL), lambda i, e: (jnp.minimum(e, ne - 1), 0)),
                  pl.BlockSpec((D_MODEL, eb), lambda i, e: (0, jnp.maximum(e - 1, 0)))],
        out_specs=pl.BlockSpec((t, D_MODEL), lambda i, e: (i, 0)),
        out_shape=jax.ShapeDtypeStruct((n, D_MODEL), F32),
        scratch_shapes=[pltpu.VMEM((t, D_MODEL), BF16),
                        pltpu.VMEM((t, kd), BF16),
                        pltpu.VMEM((PEER_N_KEYS, t), F32),
                        pltpu.VMEM((PEER_N_KEYS, t), F32),
                        sel(F32), sel(F32),
                        sel(BF16), sel(BF16),
                        pltpu.VMEM((PEER_HEADS, eb // PEER_N_KEYS, t), F32),
                        pltpu.VMEM((PEER_HEADS, eb // PEER_N_KEYS, t), F32),
                        pltpu.VMEM((eb, t), F32),
                        pltpu.VMEM((eb, t), F32),
                        pltpu.VMEM((D_MODEL, t), F32)],
        compiler_params=pltpu.CompilerParams(
            dimension_semantics=("arbitrary", "arbitrary"),
            vmem_limit_bytes=V7X_VMEM_LIMIT_BYTES),
        name="peer",
    )(x, g.reshape(1, D_MODEL).astype(F32), w_q.astype(BF16), sub_keys.astype(BF16), u, vt)


def _rmsnorm_kernel(x_ref, g_ref, o_ref):
    x = x_ref[...]
    r = lax.rsqrt(jnp.mean(x * x, axis=-1, keepdims=True) + RMS_EPS)
    o_ref[...] = x * r * g_ref[...]


def rmsnorm_rows(x, g, *, tm_pref=1024):
    n, d = x.shape
    tm = _pick_tile(n, tm_pref)
    return pl.pallas_call(
        _rmsnorm_kernel,
        grid=(n // tm,),
        in_specs=[pl.BlockSpec((tm, d), lambda i: (i, 0)), pl.BlockSpec((1, d), lambda i: (0, 0))],
        out_specs=pl.BlockSpec((tm, d), lambda i: (i, 0)),
        out_shape=jax.ShapeDtypeStruct((n, d), F32),
        name="final_rmsnorm",
    )(x, g.reshape(1, d).astype(F32))


def _forward(xs, p, depth):
    groups = []
    off = 0
    for g in xs:
        groups.append((off, g.shape[0], g.shape[1]))
        off += g.shape[0] * g.shape[1]
    x = jnp.concatenate([g.reshape(-1, D_MODEL) for g in xs], axis=0)
    tm = 512
    for _, _, s in groups:
        tm = math.gcd(tm, s)
    tables = _rope_tables(max(s for _, _, s in groups))
    ones64 = np.kron(np.eye(8, dtype=np.float32), np.ones((64, 64), np.float32))
    blockdiag = jnp.asarray(ones64, BF16)

    for i in range(depth):
        gq, gk = p['a_q_norm_g'][i].astype(F32), p['a_k_norm_g'][i].astype(F32)
        head_g = jnp.stack([jnp.tile(gq, 2), jnp.tile(gq[_SW_A], 2), jnp.tile(gk, 2), jnp.tile(gk[_SW_A], 2)]
                           + [jnp.zeros((LANES,), F32)] * 4)
        qa, ka, va, qb, kb, vb, qc, kc, vc, gates = fused_projection(
            x, groups, p['norm_mix_g'][i], _prep_w_in(p['w_in'][i]), tables, head_g, blockdiag,
            p['c_q_norm_g'][i], _prep_w_uq(p['c_w_uq'][i]), p['c_kv_norm_g'][i], _prep_w_ukv(p['c_w_ukv'][i]),
            p['gate_b'][i], tm=tm)

        ya, ob, yc = [], [], []
        for grp in groups:
            ya.append(flash_attention(qa, ka, va, grp, n_heads=A_HEADS, k_div=4, v_div=A_HEADS, dv=64,
                                      row_stride=4, row_mod=2, out_dtype=BF16))
            ob.append(flash_attention(qb, kb, vb, grp, n_heads=2 * B_HEADS, k_div=2, v_div=2, dv=LANES,
                                      row_stride=1, row_mod=1, out_dtype=F32))
            yc.append(flash_attention(qc, kc, vc, grp, n_heads=C_HEADS, k_div=1, v_div=2, dv=64,
                                      row_stride=1, row_mod=2, out_dtype=BF16))
        cat = lambda parts: parts[0] if len(parts) == 1 else jnp.concatenate(parts, axis=1)

        lam_init = 0.8 - 0.6 * math.exp(-0.3 * i)
        f32 = lambda a: a.astype(F32)
        lam = (jnp.exp(jnp.sum(f32(p['b_lambda_q1'][i]) * f32(p['b_lambda_k1'][i])))
               - jnp.exp(jnp.sum(f32(p['b_lambda_q2'][i]) * f32(p['b_lambda_k2'][i]))) + lam_init)
        scalars = jnp.stack([lam, jnp.float32(1.0 - lam_init)]).astype(F32)
        x = gated_merge(scalars, x, cat(ya), cat(ob), cat(yc), gates, p['b_subln_g'][i],
                        p['w_up_a'][i], p['w_up_b'][i], p['w_up_c'][i], p['w_out'][i], tm=tm)

        u = p['peer_u'][i].astype(BF16)
        vt = p['peer_v'][i].astype(BF16).T
        x = peer_layer(x, p['norm_ffn_g'][i], p['peer_w_q'][i], p['peer_sub_keys'][i], u, vt)

    y = rmsnorm_rows(x, p['final_norm_g'])
    return [y[off:off + b * s].reshape(b, s, D_MODEL) for off, b, s in groups]


def kernel(x_prompt, x_sample, norm_mix_g, w_in, a_q_norm_g, a_k_norm_g, b_lambda_q1, b_lambda_k1, b_lambda_q2, b_lambda_k2, b_subln_g, c_q_norm_g, c_w_uq, c_kv_norm_g, c_w_ukv, gate_b, w_up_a, w_up_b, w_up_c, w_out, norm_ffn_g, peer_w_q, peer_sub_keys, peer_u, peer_v, final_norm_g):
    p = {
        'norm_mix_g': norm_mix_g, 'w_in': w_in, 'a_q_norm_g': a_q_norm_g, 'a_k_norm_g': a_k_norm_g,
        'b_lambda_q1': b_lambda_q1, 'b_lambda_k1': b_lambda_k1, 'b_lambda_q2': b_lambda_q2,
        'b_lambda_k2': b_lambda_k2, 'b_subln_g': b_subln_g, 'c_q_norm_g': c_q_norm_g, 'c_w_uq': c_w_uq,
        'c_kv_norm_g': c_kv_norm_g, 'c_w_ukv': c_w_ukv, 'gate_b': gate_b, 'w_up_a': w_up_a,
        'w_up_b': w_up_b, 'w_up_c': w_up_c, 'w_out': w_out, 'norm_ffn_g': norm_ffn_g,
        'peer_w_q': peer_w_q, 'peer_sub_keys': peer_sub_keys, 'peer_u': peer_u, 'peer_v': peer_v,
        'final_norm_g': final_norm_g,
    }
    y_prompt, y_sample = _forward([x_prompt, x_sample], p, w_in.shape[0])
    return (y_prompt, y_sample)
```

```python
import functools
import math

import jax
import jax.numpy as jnp
import numpy as np
from jax import lax
from jax.experimental import pallas as pl
from jax.experimental.pallas import tpu as pltpu

F32 = jnp.float32
BF16 = jnp.bfloat16

D_MODEL = 1024
GRID_W = 64
RMS_EPS = 1e-6
ROPE_THETA = 500000.0
AXIAL_THETA = 10000.0

A_HEADS, A_KV_HEADS, A_HEAD_DIM = 8, 2, 64
A_AXIS_DIM = A_HEAD_DIM // 2
A_WIDTH = A_HEADS * A_HEAD_DIM
B_HEADS, B_HEAD_DIM = 4, 64
B_V_DIM = 2 * B_HEAD_DIM
B_ROPE_DIM = B_HEAD_DIM // 4
B_WIDTH = B_HEADS * B_V_DIM
C_HEADS, C_Q_RANK, C_KV_RANK = 8, 384, 256
C_NOPE_DIM, C_ROPE_DIM, C_V_DIM = 64, 32, 64
C_WIDTH = C_HEADS * C_V_DIM
N_BRANCH = 3
GATE_COLS = N_BRANCH * D_MODEL
SPLITS = (A_HEADS * A_HEAD_DIM, A_KV_HEADS * A_HEAD_DIM, A_KV_HEADS * A_HEAD_DIM,
          B_HEADS * 2 * B_HEAD_DIM, B_HEADS * 2 * B_HEAD_DIM, B_HEADS * B_V_DIM,
          C_Q_RANK, C_KV_RANK, C_ROPE_DIM, GATE_COLS)
IN_WIDTH = sum(SPLITS)
PEER_HEADS, PEER_N_KEYS, PEER_TOPK, PEER_KEY_DIM = 8, 128, 16, 256
PEER_HALF = PEER_KEY_DIM // 2
PEER_N_EXPERTS = PEER_N_KEYS * PEER_N_KEYS

V7X_VMEM_LIMIT_BYTES = 58 * 1024 * 1024
LANES = 128
LOG2E = math.log2(math.e)

_SEC_WIDTHS = (('qa', 512), ('qa_sw', 512), ('ka', 256), ('ka_sw', 256), ('va', 128),
               ('qb', 512), ('qb_sw', 512), ('kb', 512), ('kb_sw', 512), ('vb', 512),
               ('cql', C_Q_RANK), ('ckvl', C_KV_RANK), ('kr', 128), ('kr_sw', 128), ('gates', GATE_COLS))
_SEC = {}
_off = 0
for _name, _w in _SEC_WIDTHS:
    _SEC[_name] = (_off, _off + _w)
    _off += _w
W_BIG_COLS = _off


def _pick_tile(n, pref):
    t = min(n, pref)
    while n % t:
        t //= 2
    return t


def _swap_halves(n_groups, group, rope_dims, half):
    idx = np.arange(n_groups * group)
    for g in range(n_groups):
        for start in rope_dims:
            lo = g * group + start
            idx[lo:lo + half] = np.arange(lo + half, lo + 2 * half)
            idx[lo + half:lo + 2 * half] = np.arange(lo, lo + half)
    return idx


_SW_A = _swap_halves(1, 64, (0, 32), 16)
_SW_B = _swap_halves(1, 64, (0,), 8)
_SW_C = _swap_halves(1, 32, (0,), 16)


def _prep_w_in(w):
    offs = np.cumsum((0,) + SPLITS)
    aq, ak, av, bq, bk, bv, cq, ckv, kr, gt = (np.arange(offs[i], offs[i + 1]) for i in range(10))
    sw_q = np.concatenate([64 * h + _SW_A for h in range(A_HEADS)])
    sw_b = np.concatenate([64 * h + _SW_B for h in range(2 * B_HEADS)])
    ka_rep = np.concatenate([np.arange(0, 64), np.arange(0, 64), np.arange(64, 128), np.arange(64, 128)])
    ka_rep_sw = np.concatenate([_SW_A, _SW_A, 64 + _SW_A, 64 + _SW_A])
    zeros = lambda n: jnp.zeros((w.shape[0], n), w.dtype)
    kr_blk = jnp.concatenate([zeros(64), w[:, kr], zeros(32)], axis=1)
    kr_blk_sw = jnp.concatenate([zeros(64), w[:, kr[_SW_C]], zeros(32)], axis=1)
    cols = [w[:, aq], w[:, aq[sw_q]], w[:, ak[ka_rep]], w[:, ak[ka_rep_sw]], w[:, av],
            w[:, bq], w[:, bq[sw_b]], w[:, bk], w[:, bk[sw_b]], w[:, bv],
            w[:, cq], w[:, ckv], kr_blk, kr_blk_sw, w[:, gt]]
    return jnp.concatenate(cols, axis=1).astype(BF16)


def _prep_w_uq(w):
    r = w.shape[0]
    w3 = w.reshape(r, C_HEADS, C_NOPE_DIM + C_ROPE_DIM)
    z32 = jnp.zeros((r, C_HEADS, 32), w.dtype)
    z64 = jnp.zeros((r, C_HEADS, 64), w.dtype)
    main = jnp.concatenate([w3, z32], axis=-1)
    part = jnp.concatenate([z64, w3[:, :, C_NOPE_DIM + _SW_C], z32], axis=-1)
    return jnp.concatenate([main.reshape(r, -1), part.reshape(r, -1)], axis=1).astype(BF16)


def _prep_w_ukv(w):
    r = w.shape[0]
    w3 = w.reshape(r, C_HEADS, C_NOPE_DIM + C_V_DIM)
    kn = jnp.concatenate([w3[:, :, :C_NOPE_DIM], jnp.zeros((r, C_HEADS, 64), w.dtype)], axis=-1)
    return jnp.concatenate([kn.reshape(r, -1), w3[:, :, C_NOPE_DIM:].reshape(r, -1)], axis=1).astype(BF16)


def _rope_tables(s_max):
    def angles(pos, dim, theta):
        inv = 1.0 / (theta ** (jnp.arange(0, dim, 2, dtype=F32) / dim))
        ang = pos[:, None] * inv[None, :]
        return jnp.cos(ang), jnp.sin(ang)

    t = jnp.arange(s_max, dtype=F32)
    row = jnp.floor(t / GRID_W)
    col = t - row * GRID_W
    cr, sr = angles(row, A_AXIS_DIM, AXIAL_THETA)
    cc, sc = angles(col, A_AXIS_DIM, AXIAL_THETA)
    cos_a = jnp.concatenate([cr, cr, cc, cc] * 2, axis=1)
    sin_a = jnp.concatenate([-sr, sr, -sc, sc] * 2, axis=1)
    cb, sb = angles(t, B_ROPE_DIM, ROPE_THETA)
    one = lambda n: jnp.ones((s_max, n), F32)
    zero = lambda n: jnp.zeros((s_max, n), F32)
    cos_b = jnp.concatenate([cb, cb, one(48)] * 2, axis=1)
    sin_b = jnp.concatenate([-sb, sb, zero(48)] * 2, axis=1)
    c3, s3 = angles(t, C_ROPE_DIM, ROPE_THETA)
    cos_c = jnp.concatenate([one(64), c3, c3, one(32)], axis=1)
    sin_c = jnp.concatenate([zero(64), -s3, s3, zero(32)], axis=1)
    return jnp.concatenate([cos_a, sin_a, cos_b, sin_b, cos_c, sin_c], axis=1)


def _proj_kernel(x_ref, g_ref, w_ref, tab_ref, hg_ref, bd_ref, gcq_ref, wuq_ref, gckv_ref, wukv_ref, gb_ref,
                 qa_ref, ka_ref, va_ref, qb_ref, kb_ref, vb_ref, qc_ref, kc_ref, vc_ref, gt_ref):
    x = x_ref[...]
    r = lax.rsqrt(jnp.mean(x * x, axis=-1, keepdims=True) + RMS_EPS)
    h = (x * r * g_ref[...]).astype(BF16)
    tm = x.shape[0]

    def mm(name):
        lo, hi = _SEC[name]
        return jnp.dot(h, w_ref[:, lo:hi], preferred_element_type=F32)

    tab = lambda k: tab_ref[:, k * LANES:(k + 1) * LANES]
    cos_a, sin_a, cos_b, sin_b, cos_c, sin_c = (tab(k) for k in range(6))
    hg = lambda k: hg_ref[k:k + 1, :]
    gq, gq_sw, gk, gk_sw = hg(0), hg(1), hg(2), hg(3)
    blk = lambda a, j: a[:, j * LANES:(j + 1) * LANES]
    lane = lax.broadcasted_iota(jnp.int32, (tm, LANES), 1)
    half_mask = (lane < 64, lane >= 64)

    def head_rms(v):
        w = v.shape[1]
        ss = jnp.dot((v * v).astype(BF16), bd_ref[0:w, 0:w], preferred_element_type=F32)
        return lax.rsqrt(ss * (1.0 / A_HEAD_DIM) + RMS_EPS)

    qa, qa_sw = mm('qa'), mm('qa_sw')
    rq = head_rms(qa)
    for j in range(4):
        v = (blk(qa, j) * gq * cos_a + blk(qa_sw, j) * gq_sw * sin_a) * blk(rq, j) * (A_HEAD_DIM ** -0.5 * LOG2E)
        for c in range(2):
            hd = 2 * j + c
            qa_ref[:, hd * LANES:(hd + 1) * LANES] = jnp.where(half_mask[c], v, 0.0).astype(BF16)
    ka, ka_sw = mm('ka'), mm('ka_sw')
    rk = head_rms(ka)
    for j in range(2):
        v = (blk(ka, j) * gk * cos_a + blk(ka_sw, j) * gk_sw * sin_a) * blk(rk, j)
        ka_ref[:, j * LANES:(j + 1) * LANES] = v.astype(BF16)
    va = mm('va').astype(BF16)
    for hd in range(A_KV_HEADS):
        va_ref[hd] = va[:, hd * A_HEAD_DIM:(hd + 1) * A_HEAD_DIM]

    qb, qb_sw = mm('qb'), mm('qb_sw')
    for j in range(4):
        v = (blk(qb, j) * cos_b + blk(qb_sw, j) * sin_b) * (B_HEAD_DIM ** -0.5 * LOG2E)
        for c in range(2):
            hd = 2 * j + c
            qb_ref[:, hd * LANES:(hd + 1) * LANES] = jnp.where(half_mask[c], v, 0.0).astype(BF16)
    kb, kb_sw = mm('kb'), mm('kb_sw')
    for j in range(4):
        kb_ref[:, j * LANES:(j + 1) * LANES] = (blk(kb, j) * cos_b + blk(kb_sw, j) * sin_b).astype(BF16)
    vb_ref[...] = mm('vb').astype(BF16)

    def latent(name, g):
        v = mm(name)
        rr = lax.rsqrt(jnp.mean(v * v, axis=-1, keepdims=True) + RMS_EPS)
        return (v * rr * g).astype(BF16)

    cq2 = jnp.dot(latent('cql', gcq_ref[...]), wuq_ref[...], preferred_element_type=F32)
    nq = C_HEADS * LANES
    for j in range(C_HEADS):
        v = (blk(cq2, j) * cos_c + blk(cq2, C_HEADS + j) * sin_c) * ((C_NOPE_DIM + C_ROPE_DIM) ** -0.5 * LOG2E)
        qc_ref[:, j * LANES:(j + 1) * LANES] = v.astype(BF16)
    ckv2 = jnp.dot(latent('ckvl', gckv_ref[...]), wukv_ref[...], preferred_element_type=F32)
    k_rope = mm('kr') * cos_c + mm('kr_sw') * sin_c
    for j in range(C_HEADS):
        kc_ref[:, j * LANES:(j + 1) * LANES] = (blk(ckv2, j) + k_rope).astype(BF16)
    vc = ckv2[:, nq:nq + C_WIDTH].astype(BF16)
    for hd in range(C_HEADS):
        vc_ref[hd] = vc[:, hd * C_V_DIM:(hd + 1) * C_V_DIM]

    gt_ref[...] = jax.nn.sigmoid(mm('gates') + gb_ref[...]).astype(BF16)


def _pos_tile_map(groups, tm):
    def index_map(i):
        pos = jnp.int32(0)
        for off, _, s in groups:
            local = i - off // tm
            pos = jnp.where(local >= 0, local % (s // tm), pos)
        return pos, 0
    return index_map


def fused_projection(x, groups, g, w_big, tables, head_g, blockdiag, g_cq, w_uq, g_ckv, w_ukv, gate_b, *, tm):
    n = x.shape[0]
    const = lambda a: pl.BlockSpec(a.shape, lambda i: (0,) * a.ndim, pipeline_mode=pl.Buffered(1))
    row = lambda w: pl.BlockSpec((tm, w), lambda i: (i, 0))
    out_w = (1024, 256, (A_KV_HEADS, A_HEAD_DIM), 1024, 512, 512, 1024, 1024, (C_HEADS, C_V_DIM), GATE_COLS)

    def out_spec(w):
        if isinstance(w, tuple):
            return pl.BlockSpec((w[0], tm, w[1]), lambda i: (0, i, 0))
        return row(w)

    def out_struct(w):
        shape = (w[0], n, w[1]) if isinstance(w, tuple) else (n, w)
        return jax.ShapeDtypeStruct(shape, BF16)

    g2 = g.reshape(1, -1).astype(F32)
    gcq2 = g_cq.reshape(1, -1).astype(F32)
    gckv2 = g_ckv.reshape(1, -1).astype(F32)
    gb2 = gate_b.reshape(1, -1).astype(F32)
    return pl.pallas_call(
        _proj_kernel,
        grid=(n // tm,),
        in_specs=[row(D_MODEL), const(g2), const(w_big),
                  pl.BlockSpec((tm, 6 * LANES), _pos_tile_map(groups, tm)),
                  const(head_g), const(blockdiag), const(gcq2), const(w_uq), const(gckv2), const(w_ukv), const(gb2)],
        out_specs=[out_spec(w) for w in out_w],
        out_shape=[out_struct(w) for w in out_w],
        compiler_params=pltpu.CompilerParams(
            dimension_semantics=("arbitrary",),
            vmem_limit_bytes=V7X_VMEM_LIMIT_BYTES),
        name="fused_projection",
    )(x, g2, w_big, tables, head_g, blockdiag, gcq2, w_uq, gckv2, w_ukv, gb2)


def _flash_kernel(q_ref, k_ref, v_ref, o_ref, *, tk, nk):
    q = q_ref[...]
    tq = q.shape[0]
    dv = v_ref.shape[-1]
    nt = (((1,), (1,)), ((), ()))
    tn = (((0,), (0,)), ((), ()))

    def scores(c):
        return lax.dot_general(k_ref[c * tk:(c + 1) * tk, :], q, nt, preferred_element_type=F32)

    m = jnp.full((1, tq), -jnp.inf, F32)
    l = jnp.zeros((1, tq), F32)
    acc = jnp.zeros((dv, tq), F32)
    s = scores(0)
    for c in range(nk):
        s_next = scores(c + 1) if c + 1 < nk else None
        m_new = jnp.maximum(m, jnp.max(s, axis=0, keepdims=True))
        alpha = jnp.exp2(m - m_new)
        p = jnp.exp2(s - m_new)
        l = alpha * l + jnp.sum(p, axis=0, keepdims=True)
        vc = v_ref[c * tk:(c + 1) * tk, :]
        acc = alpha * acc + lax.dot_general(vc, p.astype(BF16), tn, preferred_element_type=F32)
        m, s = m_new, s_next
    o_ref[...] = (acc * (1.0 / l)).astype(o_ref.dtype)


def flash_attention(q, k, v, group, *, n_heads, k_div, v_div, out_dtype, tq_pref=2048, tk_pref=512):
    off, b, s = group
    tq = _pick_tile(s, tq_pref)
    tk = _pick_tile(s, tk_pref)
    assert off % s == 0
    qb0, kb0, nq = off // tq, off // s, s // tq
    if v.ndim == 3:
        dv = v.shape[-1]
        v_spec = pl.BlockSpec((None, s, dv), lambda bi, h, i: (h // v_div, kb0 + bi, 0))
    else:
        dv = LANES
        v_spec = pl.BlockSpec((s, LANES), lambda bi, h, i: (kb0 + bi, h // v_div))
    return pl.pallas_call(
        functools.partial(_flash_kernel, tk=tk, nk=s // tk),
        grid=(b, n_heads, nq),
        in_specs=[pl.BlockSpec((tq, LANES), lambda bi, h, i: (qb0 + bi * nq + i, h)),
                  pl.BlockSpec((s, LANES), lambda bi, h, i: (kb0 + bi, h // k_div)),
                  v_spec],
        out_specs=pl.BlockSpec((dv, tq), lambda bi, h, i: (h, bi * nq + i)),
        out_shape=jax.ShapeDtypeStruct((n_heads * dv, b * s), out_dtype),
        compiler_params=pltpu.CompilerParams(
            dimension_semantics=("arbitrary", "arbitrary", "arbitrary"),
            vmem_limit_bytes=V7X_VMEM_LIMIT_BYTES),
        name="flash_attention",
    )(q, k, v)


def _merge_kernel(sc_ref, x_ref, ya_ref, ob_ref, yc_ref, gt_ref, gs_ref, wa_ref, wb_ref, wc_ref, wo_ref, o_ref):
    lam = sc_ref[0]
    post = sc_ref[1]
    tn = (((0,), (0,)), ((), ()))
    ybs = []
    for hd in range(B_HEADS):
        o0 = ob_ref[(2 * hd) * B_V_DIM:(2 * hd + 1) * B_V_DIM, :]
        o1 = ob_ref[(2 * hd + 1) * B_V_DIM:(2 * hd + 2) * B_V_DIM, :]
        d = o0 - lam * o1
        rr = lax.rsqrt(jnp.mean(d * d, axis=0, keepdims=True) + RMS_EPS)
        ybs.append(((d * rr * gs_ref[...]) * post).astype(BF16))
    yb = jnp.concatenate(ybs, axis=0)
    ua = lax.dot_general(ya_ref[...], wa_ref[...], tn, preferred_element_type=F32)
    ub = lax.dot_general(yb, wb_ref[...], tn, preferred_element_type=F32)
    uc = lax.dot_general(yc_ref[...], wc_ref[...], tn, preferred_element_type=F32)
    gt = gt_ref[...].astype(F32)
    merged = (gt[:, 0:D_MODEL] * ua + gt[:, D_MODEL:2 * D_MODEL] * ub + gt[:, 2 * D_MODEL:3 * D_MODEL] * uc)
    o_ref[...] = x_ref[...] + jnp.dot(merged.astype(BF16), wo_ref[...], preferred_element_type=F32)


def gated_merge(scalars, x, ya_t, ob_t, yc_t, gates, subln_g, w_up_a, w_up_b, w_up_c, w_out, *, tm):
    n = x.shape[0]
    row = lambda w: pl.BlockSpec((tm, w), lambda i: (i, 0))
    col = lambda r: pl.BlockSpec((r, tm), lambda i: (0, i))
    const = lambda a: pl.BlockSpec(a.shape, lambda i: (0,) * a.ndim, pipeline_mode=pl.Buffered(1))
    ws = [w.astype(BF16) for w in (w_up_a, w_up_b, w_up_c, w_out)]
    gs = subln_g.reshape(-1, 1).astype(F32)
    return pl.pallas_call(
        _merge_kernel,
        grid=(n // tm,),
        in_specs=[pl.BlockSpec(memory_space=pltpu.SMEM),
                  row(D_MODEL), col(A_WIDTH), col(2 * B_WIDTH), col(C_WIDTH), row(GATE_COLS), const(gs)]
                 + [const(w) for w in ws],
        out_specs=row(D_MODEL),
        out_shape=jax.ShapeDtypeStruct((n, D_MODEL), F32),
        compiler_params=pltpu.CompilerParams(
            dimension_semantics=("arbitrary",),
            vmem_limit_bytes=V7X_VMEM_LIMIT_BYTES),
        name="gated_merge",
    )(scalars, x, ya_t, ob_t, yc_t, gates, gs, *ws)


SUBLANES = 8


def _oddeven_merge_sort_pairs(n):
    def merge(lo, hi, r):
        step = r * 2
        if step < hi - lo:
            yield from merge(lo, hi, step)
            yield from merge(lo + r, hi, step)
            yield from [(i, i + r) for i in range(lo + r, hi - r, step)]
        else:
            yield (lo, lo + r)

    def sort(lo, hi):
        if hi - lo >= 1:
            mid = lo + (hi - lo) // 2
            yield from sort(lo, mid)
            yield from sort(mid + 1, hi)
            yield from merge(lo, hi, 1)

    return list(sort(0, n - 1))


def _compare_exchange(xs, pairs):
    xs = list(xs)
    for i, j in pairs:
        hi, lo = jnp.maximum(xs[i], xs[j]), jnp.minimum(xs[i], xs[j])
        xs[i], xs[j] = hi, lo
    return xs


def _bitonic_merge16(xs):
    pairs = [(i, i + d) for d in (8, 4, 2, 1) for i in range(PEER_TOPK) if not i & d]
    return _compare_exchange(xs, pairs)


def _top16_columns(groups):
    xs = _compare_exchange(groups, _oddeven_merge_sort_pairs(len(groups)))
    for shift in (4, 2, 1):
        ys = [pltpu.roll(x, shift, 0) for x in xs]
        if len(xs) == 8:
            zs = xs + ys[::-1]
        else:
            zs = [jnp.maximum(xs[k], ys[PEER_TOPK - 1 - k]) for k in range(PEER_TOPK)]
        xs = _bitonic_merge16(zs)
    return xs


def _sublane_sum(x):
    for shift in (4, 2, 1):
        x = x + pltpu.roll(x, shift, 0)
    return x


def _gelu_tanh(x):
    k = math.sqrt(2.0 / math.pi)
    c1 = -2.0 * k * LOG2E
    c3 = c1 * 0.044715
    return x / (1.0 + jnp.exp2(x * (c1 + c3 * (x * x))))


def _peer_select_chunk(s1, s2):
    w = s1.shape[1]
    rows = lambda s: [s[SUBLANES * g:SUBLANES * (g + 1), :] for g in range(PEER_N_KEYS // SUBLANES)]
    g1, g2 = rows(s1), rows(s2)
    a = _top16_columns(g1)
    b = _top16_columns(g2)
    cand_rows = [[a[i] + b[j] for j in range(PEER_TOPK // (i + 1))] for i in range(PEER_TOPK)]
    cands = [c for row in cand_rows for c in row]
    sub = lax.broadcasted_iota(jnp.int32, (SUBLANES, w), 0)
    neg = jnp.full((SUBLANES, w), -jnp.inf, F32)
    packed = []
    for g in range(8):
        v = neg
        for r in range(SUBLANES):
            k = SUBLANES * g + r
            if k < len(cands):
                v = jnp.where(sub == r, cands[k], v)
        packed.append(v)
    tau = _top16_columns(packed)[PEER_TOPK - 1]
    z = sum(jnp.where(pk >= tau, jnp.exp(pk - cands[0]), 0.0) for pk in packed)
    zinv = 1.0 / _sublane_sum(z)
    partners = [sum(jnp.where(c >= tau, 1.0, 0.0) for c in row) for row in cand_rows]

    def lookup(g, keys, vals, default):
        v = jnp.full((SUBLANES, w), default, F32)
        for key, val in zip(keys, vals):
            v = jnp.where(g == key, val, v)
        return v

    c1 = jnp.concatenate([lookup(g, a, partners, 0.0) for g in g1], axis=0)
    r2 = jnp.concatenate([lookup(g, b, [float(k) for k in range(PEER_TOPK)], float(PEER_TOPK)) for g in g2], axis=0)
    e1 = jnp.concatenate([jnp.exp(g - a[0]) * zinv for g in g1], axis=0)
    e2 = jnp.concatenate([jnp.exp(g - b[0]) for g in g2], axis=0)
    return c1, e1, r2.astype(BF16), e2.astype(BF16)


def _peer_kernel(x_ref, g_ref, wq_ref, sk_ref, u_ref, vt_ref, o_ref,
                 h_scr, q_scr, s1_scr, s2_scr, c1_scr, e1_scr, r2_scr, e2_scr, c1cur_scr, e1cur_scr,
                 s_scr, sprev_scr, acc_scr,
                 *, t, eb, lc, ne):
    e = pl.program_id(1)
    nb = eb // PEER_N_KEYS
    bf16_rows = 2 * SUBLANES
    pv_rows = 2

    def key_row(ref, hh, j, cs):
        row = jnp.broadcast_to(ref[hh, j:j + 1, cs], (bf16_rows, lc)).astype(BF16)
        return jnp.concatenate([row] * (PEER_N_KEYS // bf16_rows), axis=0)

    @pl.when(e == 0)
    def _select():
        x = x_ref[...]
        r = lax.rsqrt(jnp.mean(x * x, axis=-1, keepdims=True) + RMS_EPS)
        h = (x * r * g_ref[...]).astype(BF16)
        h_scr[...] = h
        q_scr[...] = jnp.dot(h, wq_ref[...], preferred_element_type=F32).astype(BF16)
        acc_scr[...] = jnp.zeros_like(acc_scr)

        def head(hh, carry):
            col = pl.multiple_of(hh * PEER_KEY_DIM, PEER_KEY_DIM)
            q1 = q_scr[:, pl.ds(col, PEER_HALF)]
            q2 = q_scr[:, pl.ds(col + PEER_HALF, PEER_HALF)]
            nt = (((1,), (1,)), ((), ()))
            s1_scr[...] = lax.dot_general(sk_ref[0, hh], q1, nt, preferred_element_type=F32)
            s2_scr[...] = lax.dot_general(sk_ref[1, hh], q2, nt, preferred_element_type=F32)
            for c in range(t // LANES):
                cs = slice(c * LANES, (c + 1) * LANES)
                c1, e1, r2, e2 = _peer_select_chunk(s1_scr[:, cs], s2_scr[:, cs])
                c1_scr[hh, :, cs] = c1
                e1_scr[hh, :, cs] = e1
                r2_scr[hh, :, cs] = r2
                e2_scr[hh, :, cs] = e2
            return carry

        lax.fori_loop(0, PEER_HEADS, head, 0)

    n_parts = nb // pv_rows
    n_slabs = t // (2 * LANES)

    def score_matmul(slab):
        cs = slice(slab * 2 * LANES, (slab + 1) * 2 * LANES)
        s_scr[:, cs] = lax.dot_general(u_ref[...], h_scr[cs, :], (((1,), (1,)), ((), ())),
                                       preferred_element_type=F32)

    def stage_previous_block():
        row0 = pl.multiple_of((e - 1) * nb, nb)
        c1cur_scr[...] = c1_scr[:, pl.ds(row0, nb), :]
        e1cur_scr[...] = e1_scr[:, pl.ds(row0, nb), :]
        sprev_scr[...] = s_scr[...]

    def gate_and_accumulate(with_scores):
        s_prev = sprev_scr
        for part in range(n_parts):
            acts = []
            for j in range(part * pv_rows, (part + 1) * pv_rows):
                rs = slice(j * PEER_N_KEYS, (j + 1) * PEER_N_KEYS)
                cols = []
                for c in range(t // lc):
                    cs = slice(c * lc, (c + 1) * lc)
                    w = jnp.zeros((PEER_N_KEYS, lc), BF16)
                    for hh in range(PEER_HEADS):
                        c1row = key_row(c1cur_scr, hh, j, cs)
                        e1row = key_row(e1cur_scr, hh, j, cs)
                        keep = r2_scr[hh, :, cs] < c1row
                        w = w + jnp.where(keep, e2_scr[hh, :, cs], 0) * e1row
                    cols.append(_gelu_tanh(s_prev[rs, cs]).astype(BF16) * w)
                acts.append(jnp.concatenate(cols, axis=1))
            act = jnp.concatenate(acts, axis=0)
            ks = slice(part * pv_rows * PEER_N_KEYS, (part + 1) * pv_rows * PEER_N_KEYS)
            acc_scr[...] += jnp.dot(vt_ref[:, ks], act, preferred_element_type=F32)
            if with_scores and part % (n_parts // n_slabs) == 0:
                score_matmul(part // (n_parts // n_slabs))

    @pl.when(e == 0)
    def _first():
        for slab in range(n_slabs):
            score_matmul(slab)

    @pl.when((e > 0) & (e < ne))
    def _steady():
        stage_previous_block()
        gate_and_accumulate(True)

    @pl.when(e == ne)
    def _finish():
        stage_previous_block()
        gate_and_accumulate(False)
        o_ref[...] = x_ref[...] + acc_scr[...].T


def peer_layer(x, g, w_q, sub_keys, u, vt, *, t_pref=512, eb=1024):
    n = x.shape[0]
    t = _pick_tile(n, t_pref)
    lc = min(t, 2 * LANES)
    ne = PEER_N_EXPERTS // eb
    kd = PEER_HEADS * PEER_KEY_DIM
    sel = lambda dt: pltpu.VMEM((PEER_HEADS, PEER_N_KEYS, t), dt)
    const = lambda shape: pl.BlockSpec(shape, lambda i, e: (0,) * len(shape), pipeline_mode=pl.Buffered(1))
    return pl.pallas_call(
        functools.partial(_peer_kernel, t=t, eb=eb, lc=lc, ne=ne),
        grid=(n // t, ne + 1),
        in_specs=[pl.BlockSpec((t, D_MODEL), lambda i, e: (i, 0)),
                  const((1, D_MODEL)),
                  const((D_MODEL, kd)),
                  const((2, PEER_HEADS, PEER_N_KEYS, PEER_HALF)),
                  pl.BlockSpec((eb, D_MODEL), lambda i, e: (jnp.minimum(e, ne - 1), 0)),
                  pl.BlockSpec((D_MODEL, eb), lambda i, e: (0, jnp.maximum(e - 1, 0)))],
        out_specs=pl.BlockSpec((t, D_MODEL), lambda i, e: (i, 0)),
        out_shape=jax.ShapeDtypeStruct((n, D_MODEL), F32),
        scratch_shapes=[pltpu.VMEM((t, D_MODEL), BF16),
                        pltpu.VMEM((t, kd), BF16),
                        pltpu.VMEM((PEER_N_KEYS, t), F32),
                        pltpu.VMEM((PEER_N_KEYS, t), F32),
                        sel(F32), sel(F32),
                        sel(BF16), sel(BF16),
                        pltpu.VMEM((PEER_HEADS, eb // PEER_N_KEYS, t), F32),
                        pltpu.VMEM((PEER_HEADS, eb // PEER_N_KEYS, t), F32),
                        pltpu.VMEM((eb, t), F32),
                        pltpu.VMEM((eb, t), F32),
                        pltpu.VMEM((D_MODEL, t), F32)],
        compiler_params=pltpu.CompilerParams(
            dimension_semantics=("arbitrary", "arbitrary"),
            vmem_limit_bytes=V7X_VMEM_LIMIT_BYTES),
        name="peer",
    )(x, g.reshape(1, D_MODEL).astype(F32), w_q.astype(BF16), sub_keys.astype(BF16), u, vt)


def _rmsnorm_kernel(x_ref, g_ref, o_ref):
    x = x_ref[...]
    r = lax.rsqrt(jnp.mean(x * x, axis=-1, keepdims=True) + RMS_EPS)
    o_ref[...] = x * r * g_ref[...]


def rmsnorm_rows(x, g, *, tm_pref=1024):
    n, d = x.shape
    tm = _pick_tile(n, tm_pref)
    return pl.pallas_call(
        _rmsnorm_kernel,
        grid=(n // tm,),
        in_specs=[pl.BlockSpec((tm, d), lambda i: (i, 0)), pl.BlockSpec((1, d), lambda i: (0, 0))],
        out_specs=pl.BlockSpec((tm, d), lambda i: (i, 0)),
        out_shape=jax.ShapeDtypeStruct((n, d), F32),
        name="final_rmsnorm",
    )(x, g.reshape(1, d).astype(F32))


def _forward(xs, p, depth):
    groups = []
    off = 0
    for g in xs:
        groups.append((off, g.shape[0], g.shape[1]))
        off += g.shape[0] * g.shape[1]
    x = jnp.concatenate([g.reshape(-1, D_MODEL) for g in xs], axis=0)
    tm = 512
    for _, _, s in groups:
        tm = math.gcd(tm, s)
    tables = _rope_tables(max(s for _, _, s in groups))
    ones64 = np.kron(np.eye(8, dtype=np.float32), np.ones((64, 64), np.float32))
    blockdiag = jnp.asarray(ones64, BF16)

    for i in range(depth):
        gq, gk = p['a_q_norm_g'][i].astype(F32), p['a_k_norm_g'][i].astype(F32)
        head_g = jnp.stack([jnp.tile(gq, 2), jnp.tile(gq[_SW_A], 2), jnp.tile(gk, 2), jnp.tile(gk[_SW_A], 2)]
                           + [jnp.zeros((LANES,), F32)] * 4)
        qa, ka, va, qb, kb, vb, qc, kc, vc, gates = fused_projection(
            x, groups, p['norm_mix_g'][i], _prep_w_in(p['w_in'][i]), tables, head_g, blockdiag,
            p['c_q_norm_g'][i], _prep_w_uq(p['c_w_uq'][i]), p['c_kv_norm_g'][i], _prep_w_ukv(p['c_w_ukv'][i]),
            p['gate_b'][i], tm=tm)

        ya, ob, yc = [], [], []
        for grp in groups:
            ya.append(flash_attention(qa, ka, va, grp, n_heads=A_HEADS, k_div=4, v_div=4, out_dtype=BF16))
            ob.append(flash_attention(qb, kb, vb, grp, n_heads=2 * B_HEADS, k_div=2, v_div=2, out_dtype=F32))
            yc.append(flash_attention(qc, kc, vc, grp, n_heads=C_HEADS, k_div=1, v_div=1, out_dtype=BF16))
        cat = lambda parts: parts[0] if len(parts) == 1 else jnp.concatenate(parts, axis=1)

        lam_init = 0.8 - 0.6 * math.exp(-0.3 * i)
        f32 = lambda a: a.astype(F32)
        lam = (jnp.exp(jnp.sum(f32(p['b_lambda_q1'][i]) * f32(p['b_lambda_k1'][i])))
               - jnp.exp(jnp.sum(f32(p['b_lambda_q2'][i]) * f32(p['b_lambda_k2'][i]))) + lam_init)
        scalars = jnp.stack([lam, jnp.float32(1.0 - lam_init)]).astype(F32)
        x = gated_merge(scalars, x, cat(ya), cat(ob), cat(yc), gates, p['b_subln_g'][i],
                        p['w_up_a'][i], p['w_up_b'][i], p['w_up_c'][i], p['w_out'][i], tm=tm)

        u = p['peer_u'][i].astype(BF16)
        vt = p['peer_v'][i].astype(BF16).T
        x = peer_layer(x, p['norm_ffn_g'][i], p['peer_w_q'][i], p['peer_sub_keys'][i], u, vt)

    y = rmsnorm_rows(x, p['final_norm_g'])
    return [y[off:off + b * s].reshape(b, s, D_MODEL) for off, b, s in groups]


def kernel(x_prompt, x_sample, norm_mix_g, w_in, a_q_norm_g, a_k_norm_g, b_lambda_q1, b_lambda_k1, b_lambda_q2, b_lambda_k2, b_subln_g, c_q_norm_g, c_w_uq, c_kv_norm_g, c_w_ukv, gate_b, w_up_a, w_up_b, w_up_c, w_out, norm_ffn_g, peer_w_q, peer_sub_keys, peer_u, peer_v, final_norm_g):
    p = {
        'norm_mix_g': norm_mix_g, 'w_in': w_in, 'a_q_norm_g': a_q_norm_g, 'a_k_norm_g': a_k_norm_g,
        'b_lambda_q1': b_lambda_q1, 'b_lambda_k1': b_lambda_k1, 'b_lambda_q2': b_lambda_q2,
        'b_lambda_k2': b_lambda_k2, 'b_subln_g': b_subln_g, 'c_q_norm_g': c_q_norm_g, 'c_w_uq': c_w_uq,
        'c_kv_norm_g': c_kv_norm_g, 'c_w_ukv': c_w_ukv, 'gate_b': gate_b, 'w_up_a': w_up_a,
        'w_up_b': w_up_b, 'w_up_c': w_up_c, 'w_out': w_out, 'norm_ffn_g': norm_ffn_g,
        'peer_w_q': peer_w_q, 'peer_sub_keys': peer_sub_keys, 'peer_u': peer_u, 'peer_v': peer_v,
        'final_norm_g': final_norm_g,
    }
    y_prompt, y_sample = _forward([x_prompt, x_sample], p, w_in.shape[0])
    return (y_prompt, y_sample)
```

```python
import functools
import math

import jax
import jax.numpy as jnp
import numpy as np
from jax import lax
from jax.experimental import pallas as pl
from jax.experimental.pallas import tpu as pltpu

F32 = jnp.float32
BF16 = jnp.bfloat16

D_MODEL = 1024
GRID_W = 64
RMS_EPS = 1e-6
ROPE_THETA = 500000.0
AXIAL_THETA = 10000.0

A_HEADS, A_KV_HEADS, A_HEAD_DIM = 8, 2, 64
A_AXIS_DIM = A_HEAD_DIM // 2
A_WIDTH = A_HEADS * A_HEAD_DIM
B_HEADS, B_HEAD_DIM = 4, 64
B_V_DIM = 2 * B_HEAD_DIM
B_ROPE_DIM = B_HEAD_DIM // 4
B_WIDTH = B_HEADS * B_V_DIM
C_HEADS, C_Q_RANK, C_KV_RANK = 8, 384, 256
C_NOPE_DIM, C_ROPE_DIM, C_V_DIM = 64, 32, 64
C_WIDTH = C_HEADS * C_V_DIM
N_BRANCH = 3
GATE_COLS = N_BRANCH * D_MODEL
SPLITS = (A_HEADS * A_HEAD_DIM, A_KV_HEADS * A_HEAD_DIM, A_KV_HEADS * A_HEAD_DIM,
          B_HEADS * 2 * B_HEAD_DIM, B_HEADS * 2 * B_HEAD_DIM, B_HEADS * B_V_DIM,
          C_Q_RANK, C_KV_RANK, C_ROPE_DIM, GATE_COLS)
IN_WIDTH = sum(SPLITS)
PEER_HEADS, PEER_N_KEYS, PEER_TOPK, PEER_KEY_DIM = 8, 128, 16, 256
PEER_HALF = PEER_KEY_DIM // 2
PEER_N_EXPERTS = PEER_N_KEYS * PEER_N_KEYS

V7X_VMEM_LIMIT_BYTES = 58 * 1024 * 1024
LANES = 128
LOG2E = math.log2(math.e)

_SEC_WIDTHS = (('qa', 512), ('ka', 256), ('va', 128), ('qb', 512), ('kb', 512), ('vb', 512),
               ('cql', C_Q_RANK), ('ckvl', C_KV_RANK), ('kr', 128), ('gates', GATE_COLS))
_SEC = {}
_off = 0
for _name, _w in _SEC_WIDTHS:
    _SEC[_name] = (_off, _off + _w)
    _off += _w
W_BIG_COLS = _off


def _pick_tile(n, pref):
    t = min(n, pref)
    while n % t:
        t //= 2
    return t


def _swap_halves(n_groups, group, rope_dims, half):
    idx = np.arange(n_groups * group)
    for g in range(n_groups):
        for start in rope_dims:
            lo = g * group + start
            idx[lo:lo + half] = np.arange(lo + half, lo + 2 * half)
            idx[lo + half:lo + 2 * half] = np.arange(lo, lo + half)
    return idx


_SW_C = _swap_halves(1, 32, (0,), 16)


def _prep_w_in(w):
    offs = np.cumsum((0,) + SPLITS)
    aq, ak, av, bq, bk, bv, cq, ckv, kr, gt = (np.arange(offs[i], offs[i + 1]) for i in range(10))
    ka_rep = np.concatenate([np.arange(0, 64), np.arange(0, 64), np.arange(64, 128), np.arange(64, 128)])
    zeros = lambda n: jnp.zeros((w.shape[0], n), w.dtype)
    kr_blk = jnp.concatenate([zeros(64), w[:, kr], zeros(32)], axis=1)
    cols = [w[:, aq], w[:, ak[ka_rep]], w[:, av], w[:, bq], w[:, bk], w[:, bv],
            w[:, cq], w[:, ckv], kr_blk, w[:, gt]]
    return jnp.concatenate(cols, axis=1).astype(BF16)


def _prep_w_uq(w):
    r = w.shape[0]
    w3 = w.reshape(r, C_HEADS, C_NOPE_DIM + C_ROPE_DIM)
    z32 = jnp.zeros((r, C_HEADS, 32), w.dtype)
    z64 = jnp.zeros((r, C_HEADS, 64), w.dtype)
    main = jnp.concatenate([w3, z32], axis=-1)
    part = jnp.concatenate([z64, w3[:, :, C_NOPE_DIM + _SW_C], z32], axis=-1)
    return jnp.concatenate([main.reshape(r, -1), part.reshape(r, -1)], axis=1).astype(BF16)


def _prep_w_ukv(w):
    r = w.shape[0]
    w3 = w.reshape(r, C_HEADS, C_NOPE_DIM + C_V_DIM)
    kn = jnp.concatenate([w3[:, :, :C_NOPE_DIM], jnp.zeros((r, C_HEADS, 64), w.dtype)], axis=-1)
    return jnp.concatenate([kn.reshape(r, -1), w3[:, :, C_NOPE_DIM:].reshape(r, -1)], axis=1).astype(BF16)


def _rope_tables(s_max):
    def angles(pos, dim, theta):
        inv = 1.0 / (theta ** (jnp.arange(0, dim, 2, dtype=F32) / dim))
        ang = pos[:, None] * inv[None, :]
        return jnp.cos(ang), jnp.sin(ang)

    t = jnp.arange(s_max, dtype=F32)
    row = jnp.floor(t / GRID_W)
    col = t - row * GRID_W
    cr, sr = angles(row, A_AXIS_DIM, AXIAL_THETA)
    cc, sc = angles(col, A_AXIS_DIM, AXIAL_THETA)
    cos_a = jnp.concatenate([cr, cr, cc, cc] * 2, axis=1)
    sin_a = jnp.concatenate([-sr, sr, -sc, sc] * 2, axis=1)
    cb, sb = angles(t, B_ROPE_DIM, ROPE_THETA)
    one = lambda n: jnp.ones((s_max, n), F32)
    zero = lambda n: jnp.zeros((s_max, n), F32)
    cos_b = jnp.concatenate([cb, cb, one(48)] * 2, axis=1)
    sin_b = jnp.concatenate([-sb, sb, zero(48)] * 2, axis=1)
    c3, s3 = angles(t, C_ROPE_DIM, ROPE_THETA)
    cos_c = jnp.concatenate([one(64), c3, c3, one(32)], axis=1)
    sin_c = jnp.concatenate([zero(64), -s3, s3, zero(32)], axis=1)
    return jnp.concatenate([cos_a, sin_a, cos_b, sin_b, cos_c, sin_c], axis=1)


def _proj_kernel(x_ref, g_ref, w_ref, tab_ref, hg_ref, bd_ref, gcq_ref, wuq_ref, gckv_ref, wukv_ref, gb_ref,
                 qa_ref, ka_ref, va_ref, qb_ref, kb_ref, vb_ref, qc_ref, kc_ref, vc_ref, gt_ref):
    x = x_ref[...]
    r = lax.rsqrt(jnp.mean(x * x, axis=-1, keepdims=True) + RMS_EPS)
    h = (x * r * g_ref[...]).astype(BF16)
    tm = x.shape[0]

    def mm(name):
        lo, hi = _SEC[name]
        return jnp.dot(h, w_ref[:, lo:hi], preferred_element_type=F32)

    tab = lambda k: tab_ref[:, k * LANES:(k + 1) * LANES]
    cos_a, sin_a, cos_b, sin_b, cos_c, sin_c = (tab(k) for k in range(6))
    hg = lambda k: hg_ref[k:k + 1, :]
    gq, gk = hg(0), hg(1)
    blk = lambda a, j: a[:, j * LANES:(j + 1) * LANES]
    lane = lax.broadcasted_iota(jnp.int32, (tm, LANES), 1)
    half_mask = (lane < 64, lane >= 64)

    def make_partner(period, half):
        want = jnp.where(lane % period < half, lane + half, lane - half)
        take_first = pltpu.roll(lane, half, 1) == want

        def partner(v):
            return jnp.where(take_first, pltpu.roll(v, half, 1), pltpu.roll(v, LANES - half, 1))
        return partner

    partner_a = make_partner(A_AXIS_DIM, A_AXIS_DIM // 2)
    partner_b = make_partner(B_HEAD_DIM, B_ROPE_DIM // 2)
    partner_c = make_partner(C_ROPE_DIM, C_ROPE_DIM // 2)

    def head_rms(v):
        w = v.shape[1]
        ss = jnp.dot((v * v).astype(BF16), bd_ref[0:w, 0:w], preferred_element_type=F32)
        return lax.rsqrt(ss * (1.0 / A_HEAD_DIM) + RMS_EPS)

    qa = mm('qa')
    rq = head_rms(qa)
    for j in range(4):
        y = blk(qa, j) * gq
        v = (y * cos_a + partner_a(y) * sin_a) * blk(rq, j) * (A_HEAD_DIM ** -0.5 * LOG2E)
        for c in range(2):
            hd = 2 * j + c
            qa_ref[:, hd * LANES:(hd + 1) * LANES] = jnp.where(half_mask[c], v, 0.0).astype(BF16)
    ka = mm('ka')
    rk = head_rms(ka)
    for j in range(2):
        y = blk(ka, j) * gk
        v = (y * cos_a + partner_a(y) * sin_a) * blk(rk, j)
        ka_ref[:, j * LANES:(j + 1) * LANES] = v.astype(BF16)
    va = mm('va').astype(BF16)
    for hd in range(A_KV_HEADS):
        va_ref[hd] = va[:, hd * A_HEAD_DIM:(hd + 1) * A_HEAD_DIM]

    qb = mm('qb')
    for j in range(4):
        y = blk(qb, j)
        v = (y * cos_b + partner_b(y) * sin_b) * (B_HEAD_DIM ** -0.5 * LOG2E)
        for c in range(2):
            hd = 2 * j + c
            qb_ref[:, hd * LANES:(hd + 1) * LANES] = jnp.where(half_mask[c], v, 0.0).astype(BF16)
    kb = mm('kb')
    for j in range(4):
        y = blk(kb, j)
        kb_ref[:, j * LANES:(j + 1) * LANES] = (y * cos_b + partner_b(y) * sin_b).astype(BF16)
    vb_ref[...] = mm('vb').astype(BF16)

    def latent(name, g):
        v = mm(name)
        rr = lax.rsqrt(jnp.mean(v * v, axis=-1, keepdims=True) + RMS_EPS)
        return (v * rr * g).astype(BF16)

    cq2 = jnp.dot(latent('cql', gcq_ref[...]), wuq_ref[...], preferred_element_type=F32)
    nq = C_HEADS * LANES
    for j in range(C_HEADS):
        v = (blk(cq2, j) * cos_c + blk(cq2, C_HEADS + j) * sin_c) * ((C_NOPE_DIM + C_ROPE_DIM) ** -0.5 * LOG2E)
        qc_ref[:, j * LANES:(j + 1) * LANES] = v.astype(BF16)
    ckv2 = jnp.dot(latent('ckvl', gckv_ref[...]), wukv_ref[...], preferred_element_type=F32)
    kr = mm('kr')
    k_rope = kr * cos_c + partner_c(kr) * sin_c
    for j in range(C_HEADS):
        kc_ref[:, j * LANES:(j + 1) * LANES] = (blk(ckv2, j) + k_rope).astype(BF16)
    vc = ckv2[:, nq:nq + C_WIDTH].astype(BF16)
    for hd in range(C_HEADS):
        vc_ref[hd] = vc[:, hd * C_V_DIM:(hd + 1) * C_V_DIM]

    gt_ref[...] = jax.nn.sigmoid(mm('gates') + gb_ref[...]).astype(BF16)


def _pos_tile_map(groups, tm):
    def index_map(i):
        pos = jnp.int32(0)
        for off, _, s in groups:
            local = i - off // tm
            pos = jnp.where(local >= 0, local % (s // tm), pos)
        return pos, 0
    return index_map


def fused_projection(x, groups, g, w_big, tables, head_g, blockdiag, g_cq, w_uq, g_ckv, w_ukv, gate_b, *, tm):
    n = x.shape[0]
    const = lambda a: pl.BlockSpec(a.shape, lambda i: (0,) * a.ndim, pipeline_mode=pl.Buffered(1))
    row = lambda w: pl.BlockSpec((tm, w), lambda i: (i, 0))
    out_w = (1024, 256, (A_KV_HEADS, A_HEAD_DIM), 1024, 512, 512, 1024, 1024, (C_HEADS, C_V_DIM), GATE_COLS)

    def out_spec(w):
        if isinstance(w, tuple):
            return pl.BlockSpec((w[0], tm, w[1]), lambda i: (0, i, 0))
        return row(w)

    def out_struct(w):
        shape = (w[0], n, w[1]) if isinstance(w, tuple) else (n, w)
        return jax.ShapeDtypeStruct(shape, BF16)

    g2 = g.reshape(1, -1).astype(F32)
    gcq2 = g_cq.reshape(1, -1).astype(F32)
    gckv2 = g_ckv.reshape(1, -1).astype(F32)
    gb2 = gate_b.reshape(1, -1).astype(F32)
    return pl.pallas_call(
        _proj_kernel,
        grid=(n // tm,),
        in_specs=[row(D_MODEL), const(g2), const(w_big),
                  pl.BlockSpec((tm, 6 * LANES), _pos_tile_map(groups, tm)),
                  const(head_g), const(blockdiag), const(gcq2), const(w_uq), const(gckv2), const(w_ukv), const(gb2)],
        out_specs=[out_spec(w) for w in out_w],
        out_shape=[out_struct(w) for w in out_w],
        compiler_params=pltpu.CompilerParams(
            dimension_semantics=("arbitrary",),
            vmem_limit_bytes=V7X_VMEM_LIMIT_BYTES),
        name="fused_projection",
    )(x, g2, w_big, tables, head_g, blockdiag, gcq2, w_uq, gckv2, w_ukv, gb2)


def _flash_kernel(q_ref, k_ref, v_ref, o_ref, *, tk, nk):
    q = q_ref[...]
    tq = q.shape[0]
    dv = v_ref.shape[-1]
    nt = (((1,), (1,)), ((), ()))
    tn = (((0,), (0,)), ((), ()))

    def scores(c):
        return lax.dot_general(k_ref[c * tk:(c + 1) * tk, :], q, nt, preferred_element_type=F32)

    m = jnp.full((1, tq), -jnp.inf, F32)
    l = jnp.zeros((1, tq), F32)
    acc = jnp.zeros((dv, tq), F32)
    s = scores(0)
    for c in range(nk):
        s_next = scores(c + 1) if c + 1 < nk else None
        m_new = jnp.maximum(m, jnp.max(s, axis=0, keepdims=True))
        alpha = jnp.exp2(m - m_new)
        p = jnp.exp2(s - m_new)
        l = alpha * l + jnp.sum(p, axis=0, keepdims=True)
        vc = v_ref[c * tk:(c + 1) * tk, :]
        acc = alpha * acc + lax.dot_general(vc, p.astype(BF16), tn, preferred_element_type=F32)
        m, s = m_new, s_next
    o_ref[...] = (acc * (1.0 / l)).astype(o_ref.dtype)


def flash_attention(q, k, v, group, *, n_heads, k_div, v_div, out_dtype, tq_pref=2048, tk_pref=512):
    off, b, s = group
    tq = _pick_tile(s, tq_pref)
    tk = _pick_tile(s, tk_pref)
    assert off % s == 0
    qb0, kb0, nq = off // tq, off // s, s // tq
    if v.ndim == 3:
        dv = v.shape[-1]
        v_spec = pl.BlockSpec((None, s, dv), lambda bi, h, i: (h // v_div, kb0 + bi, 0))
    else:
        dv = LANES
        v_spec = pl.BlockSpec((s, LANES), lambda bi, h, i: (kb0 + bi, h // v_div))
    return pl.pallas_call(
        functools.partial(_flash_kernel, tk=tk, nk=s // tk),
        grid=(b, n_heads, nq),
        in_specs=[pl.BlockSpec((tq, LANES), lambda bi, h, i: (qb0 + bi * nq + i, h)),
                  pl.BlockSpec((s, LANES), lambda bi, h, i: (kb0 + bi, h // k_div)),
                  v_spec],
        out_specs=pl.BlockSpec((dv, tq), lambda bi, h, i: (h, bi * nq + i)),
        out_shape=jax.ShapeDtypeStruct((n_heads * dv, b * s), out_dtype),
        compiler_params=pltpu.CompilerParams(
            dimension_semantics=("arbitrary", "arbitrary", "arbitrary"),
            vmem_limit_bytes=V7X_VMEM_LIMIT_BYTES),
        name="flash_attention",
    )(q, k, v)


def _merge_kernel(sc_ref, x_ref, ya_ref, ob_ref, yc_ref, gt_ref, gs_ref, wa_ref, wb_ref, wc_ref, wo_ref, o_ref):
    lam = sc_ref[0]
    post = sc_ref[1]
    tn = (((0,), (0,)), ((), ()))
    ybs = []
    for hd in range(B_HEADS):
        o0 = ob_ref[(2 * hd) * B_V_DIM:(2 * hd + 1) * B_V_DIM, :]
        o1 = ob_ref[(2 * hd + 1) * B_V_DIM:(2 * hd + 2) * B_V_DIM, :]
        d = o0 - lam * o1
        rr = lax.rsqrt(jnp.mean(d * d, axis=0, keepdims=True) + RMS_EPS)
        ybs.append(((d * rr * gs_ref[...]) * post).astype(BF16))
    yb = jnp.concatenate(ybs, axis=0)
    ua = lax.dot_general(ya_ref[...], wa_ref[...], tn, preferred_element_type=F32)
    ub = lax.dot_general(yb, wb_ref[...], tn, preferred_element_type=F32)
    uc = lax.dot_general(yc_ref[...], wc_ref[...], tn, preferred_element_type=F32)
    gt = gt_ref[...].astype(F32)
    merged = (gt[:, 0:D_MODEL] * ua + gt[:, D_MODEL:2 * D_MODEL] * ub + gt[:, 2 * D_MODEL:3 * D_MODEL] * uc)
    o_ref[...] = x_ref[...] + jnp.dot(merged.astype(BF16), wo_ref[...], preferred_element_type=F32)


def gated_merge(scalars, x, ya_t, ob_t, yc_t, gates, subln_g, w_up_a, w_up_b, w_up_c, w_out, *, tm):
    n = x.shape[0]
    row = lambda w: pl.BlockSpec((tm, w), lambda i: (i, 0))
    col = lambda r: pl.BlockSpec((r, tm), lambda i: (0, i))
    const = lambda a: pl.BlockSpec(a.shape, lambda i: (0,) * a.ndim, pipeline_mode=pl.Buffered(1))
    ws = [w.astype(BF16) for w in (w_up_a, w_up_b, w_up_c, w_out)]
    gs = subln_g.reshape(-1, 1).astype(F32)
    return pl.pallas_call(
        _merge_kernel,
        grid=(n // tm,),
        in_specs=[pl.BlockSpec(memory_space=pltpu.SMEM),
                  row(D_MODEL), col(A_WIDTH), col(2 * B_WIDTH), col(C_WIDTH), row(GATE_COLS), const(gs)]
                 + [const(w) for w in ws],
        out_specs=row(D_MODEL),
        out_shape=jax.ShapeDtypeStruct((n, D_MODEL), F32),
        compiler_params=pltpu.CompilerParams(
            dimension_semantics=("arbitrary",),
            vmem_limit_bytes=V7X_VMEM_LIMIT_BYTES),
        name="gated_merge",
    )(scalars, x, ya_t, ob_t, yc_t, gates, gs, *ws)


SUBLANES = 8


def _oddeven_merge_sort_pairs(n):
    def merge(lo, hi, r):
        step = r * 2
        if step < hi - lo:
            yield from merge(lo, hi, step)
            yield from merge(lo + r, hi, step)
            yield from [(i, i + r) for i in range(lo + r, hi - r, step)]
        else:
            yield (lo, lo + r)

    def sort(lo, hi):
        if hi - lo >= 1:
            mid = lo + (hi - lo) // 2
            yield from sort(lo, mid)
            yield from sort(mid + 1, hi)
            yield from merge(lo, hi, 1)

    return list(sort(0, n - 1))


def _compare_exchange(xs, pairs):
    xs = list(xs)
    for i, j in pairs:
        hi, lo = jnp.maximum(xs[i], xs[j]), jnp.minimum(xs[i], xs[j])
        xs[i], xs[j] = hi, lo
    return xs


def _bitonic_merge16(xs):
    pairs = [(i, i + d) for d in (8, 4, 2, 1) for i in range(PEER_TOPK) if not i & d]
    return _compare_exchange(xs, pairs)


def _top16_columns(groups):
    xs = _compare_exchange(groups, _oddeven_merge_sort_pairs(len(groups)))
    for shift in (4, 2, 1):
        ys = [pltpu.roll(x, shift, 0) for x in xs]
        if len(xs) == 8:
            zs = xs + ys[::-1]
        else:
            zs = [jnp.maximum(xs[k], ys[PEER_TOPK - 1 - k]) for k in range(PEER_TOPK)]
        xs = _bitonic_merge16(zs)
    return xs


def _sublane_sum(x):
    for shift in (4, 2, 1):
        x = x + pltpu.roll(x, shift, 0)
    return x


def _gelu_tanh(x):
    k = math.sqrt(2.0 / math.pi)
    c1 = -2.0 * k * LOG2E
    c3 = c1 * 0.044715
    return x / (1.0 + jnp.exp2(x * (c1 + c3 * (x * x))))


def _peer_select_chunk(s1, s2):
    w = s1.shape[1]
    rows = lambda s: [s[SUBLANES * g:SUBLANES * (g + 1), :] for g in range(PEER_N_KEYS // SUBLANES)]
    g1, g2 = rows(s1), rows(s2)
    a = _top16_columns(g1)
    b = _top16_columns(g2)
    cand_rows = [[a[i] + b[j] for j in range(PEER_TOPK // (i + 1))] for i in range(PEER_TOPK)]
    cands = [c for row in cand_rows for c in row]
    sub = lax.broadcasted_iota(jnp.int32, (SUBLANES, w), 0)
    neg = jnp.full((SUBLANES, w), -jnp.inf, F32)
    packed = []
    for g in range(8):
        v = neg
        for r in range(SUBLANES):
            k = SUBLANES * g + r
            if k < len(cands):
                v = jnp.where(sub == r, cands[k], v)
        packed.append(v)
    tau = _top16_columns(packed)[PEER_TOPK - 1]
    z = sum(jnp.where(pk >= tau, jnp.exp(pk - cands[0]), 0.0) for pk in packed)
    zinv = 1.0 / _sublane_sum(z)
    partners = [sum(jnp.where(c >= tau, 1.0, 0.0) for c in row) for row in cand_rows]

    def lookup(g, keys, vals, default):
        v = jnp.full((SUBLANES, w), default, F32)
        for key, val in zip(keys, vals):
            v = jnp.where(g == key, val, v)
        return v

    c1 = jnp.concatenate([lookup(g, a, partners, 0.0) for g in g1], axis=0)
    r2 = jnp.concatenate([lookup(g, b, [float(k) for k in range(PEER_TOPK)], float(PEER_TOPK)) for g in g2], axis=0)
    e1 = jnp.concatenate([jnp.exp(g - a[0]) * zinv for g in g1], axis=0)
    e2 = jnp.concatenate([jnp.exp(g - b[0]) for g in g2], axis=0)
    return c1, e1, r2.astype(BF16), e2.astype(BF16)


def _peer_kernel(x_ref, g_ref, wq_ref, sk_ref, u_ref, vt_ref, o_ref,
                 h_scr, q_scr, s1_scr, s2_scr, c1_scr, e1_scr, r2_scr, e2_scr, c1cur_scr, e1cur_scr,
                 s_scr, sprev_scr, acc_scr,
                 *, t, eb, lc, ne):
    e = pl.program_id(1)
    nb = eb // PEER_N_KEYS
    bf16_rows = 2 * SUBLANES
    pv_rows = 2

    def key_row(ref, hh, j, cs):
        tiles = []
        for k in range(cs.start, cs.stop, LANES):
            r8 = jnp.broadcast_to(ref[hh, j:j + 1, k:k + LANES], (SUBLANES, LANES))
            tiles.append(jnp.concatenate([r8, r8], axis=0).astype(BF16))
        row = jnp.concatenate(tiles, axis=1)
        return jnp.concatenate([row] * (PEER_N_KEYS // bf16_rows), axis=0)

    @pl.when(e == 0)
    def _select():
        x = x_ref[...]
        r = lax.rsqrt(jnp.mean(x * x, axis=-1, keepdims=True) + RMS_EPS)
        h = (x * r * g_ref[...]).astype(BF16)
        h_scr[...] = h
        q_scr[...] = jnp.dot(h, wq_ref[...], preferred_element_type=F32).astype(BF16)
        acc_scr[...] = jnp.zeros_like(acc_scr)

        def head(hh, carry):
            col = pl.multiple_of(hh * PEER_KEY_DIM, PEER_KEY_DIM)
            q1 = q_scr[:, pl.ds(col, PEER_HALF)]
            q2 = q_scr[:, pl.ds(col + PEER_HALF, PEER_HALF)]
            nt = (((1,), (1,)), ((), ()))
            s1_scr[...] = lax.dot_general(sk_ref[0, hh], q1, nt, preferred_element_type=F32)
            s2_scr[...] = lax.dot_general(sk_ref[1, hh], q2, nt, preferred_element_type=F32)
            for c in range(t // LANES):
                cs = slice(c * LANES, (c + 1) * LANES)
                c1, e1, r2, e2 = _peer_select_chunk(s1_scr[:, cs], s2_scr[:, cs])
                c1_scr[hh, :, cs] = c1
                e1_scr[hh, :, cs] = e1
                r2_scr[hh, :, cs] = r2
                e2_scr[hh, :, cs] = e2
            return carry

        lax.fori_loop(0, PEER_HEADS, head, 0)

    n_parts = nb // pv_rows
    n_slabs = t // (2 * LANES)

    def score_matmul(slab):
        cs = slice(slab * 2 * LANES, (slab + 1) * 2 * LANES)
        s_scr[:, cs] = lax.dot_general(u_ref[...], h_scr[cs, :], (((1,), (1,)), ((), ())),
                                       preferred_element_type=F32)

    def stage_previous_block():
        row0 = pl.multiple_of((e - 1) * nb, nb)
        c1cur_scr[...] = c1_scr[:, pl.ds(row0, nb), :]
        e1cur_scr[...] = e1_scr[:, pl.ds(row0, nb), :]
        sprev_scr[...] = s_scr[...]

    def gate_and_accumulate(with_scores):
        s_prev = sprev_scr
        for part in range(n_parts):
            acts = []
            for j in range(part * pv_rows, (part + 1) * pv_rows):
                rs = slice(j * PEER_N_KEYS, (j + 1) * PEER_N_KEYS)
                cols = []
                for c in range(t // lc):
                    cs = slice(c * lc, (c + 1) * lc)
                    w = jnp.zeros((PEER_N_KEYS, lc), BF16)
                    for hh in range(PEER_HEADS):
                        c1row = key_row(c1cur_scr, hh, j, cs)
                        e1row = key_row(e1cur_scr, hh, j, cs)
                        keep = r2_scr[hh, :, cs] < c1row
                        w = w + jnp.where(keep, e2_scr[hh, :, cs], 0) * e1row
                    cols.append(_gelu_tanh(s_prev[rs, cs]).astype(BF16) * w)
                acts.append(jnp.concatenate(cols, axis=1))
            act = jnp.concatenate(acts, axis=0)
            ks = slice(part * pv_rows * PEER_N_KEYS, (part + 1) * pv_rows * PEER_N_KEYS)
            acc_scr[...] += jnp.dot(vt_ref[:, ks], act, preferred_element_type=F32)
            if with_scores and part % (n_parts // n_slabs) == 0:
                score_matmul(part // (n_parts // n_slabs))

    @pl.when(e == 0)
    def _first():
        for slab in range(n_slabs):
            score_matmul(slab)

    @pl.when((e > 0) & (e < ne))
    def _steady():
        stage_previous_block()
        gate_and_accumulate(True)

    @pl.when(e == ne)
    def _finish():
        stage_previous_block()
        gate_and_accumulate(False)
        o_ref[...] = x_ref[...] + acc_scr[...].T


def peer_layer(x, g, w_q, sub_keys, u, vt, *, t_pref=512, eb=1024):
    n = x.shape[0]
    t = _pick_tile(n, t_pref)
    lc = min(t, 2 * LANES)
    ne = PEER_N_EXPERTS // eb
    kd = PEER_HEADS * PEER_KEY_DIM
    sel = lambda dt: pltpu.VMEM((PEER_HEADS, PEER_N_KEYS, t), dt)
    const = lambda shape: pl.BlockSpec(shape, lambda i, e: (0,) * len(shape), pipeline_mode=pl.Buffered(1))
    return pl.pallas_call(
        functools.partial(_peer_kernel, t=t, eb=eb, lc=lc, ne=ne),
        grid=(n // t, ne + 1),
        in_specs=[pl.BlockSpec((t, D_MODEL), lambda i, e: (i, 0)),
                  const((1, D_MODEL)),
                  const((D_MODEL, kd)),
                  const((2, PEER_HEADS, PEER_N_KEYS, PEER_HALF)),
                  pl.BlockSpec((eb, D_MODEL), lambda i, e: (jnp.minimum(e, ne - 1), 0)),
                  pl.BlockSpec((D_MODEL, eb), lambda i, e: (0, jnp.maximum(e - 1, 0)))],
        out_specs=pl.BlockSpec((t, D_MODEL), lambda i, e: (i, 0)),
        out_shape=jax.ShapeDtypeStruct((n, D_MODEL), F32),
        scratch_shapes=[pltpu.VMEM((t, D_MODEL), BF16),
                        pltpu.VMEM((t, kd), BF16),
                        pltpu.VMEM((PEER_N_KEYS, t), F32),
                        pltpu.VMEM((PEER_N_KEYS, t), F32),
                        sel(F32), sel(F32),
                        sel(BF16), sel(BF16),
                        pltpu.VMEM((PEER_HEADS, eb // PEER_N_KEYS, t), F32),
                        pltpu.VMEM((PEER_HEADS, eb // PEER_N_KEYS, t), F32),
                        pltpu.VMEM((eb, t), F32),
                        pltpu.VMEM((eb, t), F32),
                        pltpu.VMEM((D_MODEL, t), F32)],
        compiler_params=pltpu.CompilerParams(
            dimension_semantics=("arbitrary", "arbitrary"),
            vmem_limit_bytes=V7X_VMEM_LIMIT_BYTES),
        name="peer",
    )(x, g.reshape(1, D_MODEL).astype(F32), w_q.astype(BF16), sub_keys.astype(BF16), u, vt)


def _rmsnorm_kernel(x_ref, g_ref, o_ref):
    x = x_ref[...]
    r = lax.rsqrt(jnp.mean(x * x, axis=-1, keepdims=True) + RMS_EPS)
    o_ref[...] = x * r * g_ref[...]


def rmsnorm_rows(x, g, *, tm_pref=1024):
    n, d = x.shape
    tm = _pick_tile(n, tm_pref)
    return pl.pallas_call(
        _rmsnorm_kernel,
        grid=(n // tm,),
        in_specs=[pl.BlockSpec((tm, d), lambda i: (i, 0)), pl.BlockSpec((1, d), lambda i: (0, 0))],
        out_specs=pl.BlockSpec((tm, d), lambda i: (i, 0)),
        out_shape=jax.ShapeDtypeStruct((n, d), F32),
        name="final_rmsnorm",
    )(x, g.reshape(1, d).astype(F32))


def _forward(xs, p, depth):
    groups = []
    off = 0
    for g in xs:
        groups.append((off, g.shape[0], g.shape[1]))
        off += g.shape[0] * g.shape[1]
    x = jnp.concatenate([g.reshape(-1, D_MODEL) for g in xs], axis=0)
    tm = 512
    for _, _, s in groups:
        tm = math.gcd(tm, s)
    tables = _rope_tables(max(s for _, _, s in groups))
    ones64 = np.kron(np.eye(8, dtype=np.float32), np.ones((64, 64), np.float32))
    blockdiag = jnp.asarray(ones64, BF16)

    for i in range(depth):
        gq, gk = p['a_q_norm_g'][i].astype(F32), p['a_k_norm_g'][i].astype(F32)
        head_g = jnp.stack([jnp.tile(gq, 2), jnp.tile(gk, 2)] + [jnp.zeros((LANES,), F32)] * 6)
        qa, ka, va, qb, kb, vb, qc, kc, vc, gates = fused_projection(
            x, groups, p['norm_mix_g'][i], _prep_w_in(p['w_in'][i]), tables, head_g, blockdiag,
            p['c_q_norm_g'][i], _prep_w_uq(p['c_w_uq'][i]), p['c_kv_norm_g'][i], _prep_w_ukv(p['c_w_ukv'][i]),
            p['gate_b'][i], tm=tm)

        ya, ob, yc = [], [], []
        for grp in groups:
            ya.append(flash_attention(qa, ka, va, grp, n_heads=A_HEADS, k_div=4, v_div=4, out_dtype=BF16))
            ob.append(flash_attention(qb, kb, vb, grp, n_heads=2 * B_HEADS, k_div=2, v_div=2, out_dtype=F32))
            yc.append(flash_attention(qc, kc, vc, grp, n_heads=C_HEADS, k_div=1, v_div=1, out_dtype=BF16))
        cat = lambda parts: parts[0] if len(parts) == 1 else jnp.concatenate(parts, axis=1)

        lam_init = 0.8 - 0.6 * math.exp(-0.3 * i)
        f32 = lambda a: a.astype(F32)
        lam = (jnp.exp(jnp.sum(f32(p['b_lambda_q1'][i]) * f32(p['b_lambda_k1'][i])))
               - jnp.exp(jnp.sum(f32(p['b_lambda_q2'][i]) * f32(p['b_lambda_k2'][i]))) + lam_init)
        scalars = jnp.stack([lam, jnp.float32(1.0 - lam_init)]).astype(F32)
        x = gated_merge(scalars, x, cat(ya), cat(ob), cat(yc), gates, p['b_subln_g'][i],
                        p['w_up_a'][i], p['w_up_b'][i], p['w_up_c'][i], p['w_out'][i], tm=tm)

        u = p['peer_u'][i].astype(BF16)
        vt = p['peer_v'][i].astype(BF16).T
        x = peer_layer(x, p['norm_ffn_g'][i], p['peer_w_q'][i], p['peer_sub_keys'][i], u, vt)

    y = rmsnorm_rows(x, p['final_norm_g'])
    return [y[off:off + b * s].reshape(b, s, D_MODEL) for off, b, s in groups]


def kernel(x_prompt, x_sample, norm_mix_g, w_in, a_q_norm_g, a_k_norm_g, b_lambda_q1, b_lambda_k1, b_lambda_q2, b_lambda_k2, b_subln_g, c_q_norm_g, c_w_uq, c_kv_norm_g, c_w_ukv, gate_b, w_up_a, w_up_b, w_up_c, w_out, norm_ffn_g, peer_w_q, peer_sub_keys, peer_u, peer_v, final_norm_g):
    p = {
        'norm_mix_g': norm_mix_g, 'w_in': w_in, 'a_q_norm_g': a_q_norm_g, 'a_k_norm_g': a_k_norm_g,
        'b_lambda_q1': b_lambda_q1, 'b_lambda_k1': b_lambda_k1, 'b_lambda_q2': b_lambda_q2,
        'b_lambda_k2': b_lambda_k2, 'b_subln_g': b_subln_g, 'c_q_norm_g': c_q_norm_g, 'c_w_uq': c_w_uq,
        'c_kv_norm_g': c_kv_norm_g, 'c_w_ukv': c_w_ukv, 'gate_b': gate_b, 'w_up_a': w_up_a,
        'w_up_b': w_up_b, 'w_up_c': w_up_c, 'w_out': w_out, 'norm_ffn_g': norm_ffn_g,
        'peer_w_q': peer_w_q, 'peer_sub_keys': peer_sub_keys, 'peer_u': peer_u, 'peer_v': peer_v,
        'final_norm_g': final_norm_g,
    }
    y_prompt, y_sample = _forward([x_prompt, x_sample], p, w_in.shape[0])
    return (y_prompt, y_sample)
```
